```python
import math
import jax, jax.numpy as jnp
from jax import lax
import numpy as np

D_MODEL = 1024
BATCH = 2
SEQ = 8192
DEPTH = 1
DEC_BATCH = 128
DEC_SEQ = 1
PAST_LEN = 16384
PAGE_SIZE = 128

HEAD_DIM = 64
N_HEADS_ATTN = 8
N_KV_HEADS = 2
GQA_GROUP = N_HEADS_ATTN // N_KV_HEADS
WINDOW = 128
BLOCK = WINDOW
ATTN_WIDTH = N_HEADS_ATTN * HEAD_DIM
KV_WIDTH = N_KV_HEADS * HEAD_DIM
N_HEADS_GMLP = 8
GMLP_WIDTH = N_HEADS_GMLP * HEAD_DIM
CHUNK = 128
MIX_WIDTH = ATTN_WIDTH + GMLP_WIDTH
IN_WIDTH = ATTN_WIDTH + 2 * KV_WIDTH + 2 * GMLP_WIDTH
SPLITS = [ATTN_WIDTH, ATTN_WIDTH + KV_WIDTH, ATTN_WIDTH + 2 * KV_WIDTH, ATTN_WIDTH + 2 * KV_WIDTH + GMLP_WIDTH]
NUM_BUCKETS = 32
MAX_DISTANCE = 128
D_FF = 2816
CONV_WIDTH = 3
EPS = 1e-6
NEG_INF = -1e30

kernel_name = "hymba_gmlp_swa_sink_convffn_step"


def _rms_norm(x, g):
    xf = x.astype(jnp.float32)
    y = xf * lax.rsqrt(jnp.mean(xf * xf, axis=-1, keepdims=True) + EPS)
    return (y * g.astype(jnp.float32)).astype(x.dtype)


def _layer_norm(x, g, b):
    xf = x.astype(jnp.float32)
    mu = jnp.mean(xf, axis=-1, keepdims=True)
    xc = xf - mu
    y = xc * lax.rsqrt(jnp.mean(xc * xc, axis=-1, keepdims=True) + EPS)
    return (y * g.astype(jnp.float32) + b.astype(jnp.float32)).astype(x.dtype)


def _t5_bucket(dist):
    n = jnp.maximum(dist, 0)
    max_exact = NUM_BUCKETS // 2
    nf = jnp.maximum(n, 1).astype(jnp.float32)
    large = max_exact + (jnp.log(nf / max_exact) / math.log(MAX_DISTANCE / max_exact)
                         * (NUM_BUCKETS - max_exact)).astype(jnp.int32)
    large = jnp.minimum(large, NUM_BUCKETS - 1)
    return jnp.where(n < max_exact, n, large)


def _sink_window_attn(q, k, v, dist, key_valid, sinks, rel_bias):
    N, Lq = q.shape[0], q.shape[1]
    Lk = k.shape[1]
    qg = q.reshape(N, Lq, N_KV_HEADS, GQA_GROUP, HEAD_DIM)
    logits = jnp.einsum('nqkgd,nskd->nkgqs', qg, k).astype(jnp.float32) * (HEAD_DIM ** -0.5)
    bias = rel_bias[_t5_bucket(dist)].astype(jnp.float32)
    bias = bias.transpose(2, 0, 1).reshape(N_KV_HEADS, GQA_GROUP, Lq, Lk)
    allowed = ((dist >= 0) & (dist <= WINDOW))[None] & key_valid[:, None, :]
    logits = jnp.where(allowed[:, None, None], logits + bias, NEG_INF)
    sink = jnp.broadcast_to(sinks.astype(jnp.float32).reshape(N_KV_HEADS, GQA_GROUP, 1, 1),
                            (N, N_KV_HEADS, GQA_GROUP, Lq, 1))
    probs = jax.nn.softmax(jnp.concatenate([logits, sink], axis=-1), axis=-1)[..., :-1]
    out = jnp.einsum('nkgqs,nskd->nqkgd', probs.astype(v.dtype), v)
    return out.reshape(N, Lq, ATTN_WIDTH)


def _attn_prompt(q, k, v, sinks, rel_bias):
    B, L = q.shape[0], q.shape[1]
    nb = L // BLOCK
    qb = q.reshape(B * nb, BLOCK, N_HEADS_ATTN, HEAD_DIM)

    def band(t):
        tb = t.reshape(B, nb, BLOCK, N_KV_HEADS, HEAD_DIM)
        prev = jnp.concatenate([jnp.zeros_like(tb[:, :1]), tb[:, :-1]], axis=1)
        return jnp.concatenate([prev, tb], axis=2).reshape(B * nb, 2 * BLOCK, N_KV_HEADS, HEAD_DIM)

    r = jnp.arange(BLOCK)[:, None]
    c = jnp.arange(2 * BLOCK)[None, :]
    dist = r + BLOCK - c
    blk = jnp.tile(jnp.arange(nb), B)
    key_valid = (blk[:, None] > 0) | (c >= BLOCK)
    out = _sink_window_attn(qb, band(k), band(v), dist, key_valid, sinks, rel_bias)
    return out.reshape(B, L, ATTN_WIDTH)


def _attn_sample(q, k, v, win_k, win_v, sinks, rel_bias):
    N, Ld = q.shape[0], q.shape[1]
    wb = win_k.shape[1]
    kk = jnp.concatenate([win_k, k], axis=1)
    vv = jnp.concatenate([win_v, v], axis=1)
    dist = jnp.arange(Ld)[:, None] + wb - jnp.arange(wb + Ld)[None, :]
    key_valid = jnp.ones((N, wb + Ld), dtype=bool)
    out = _sink_window_attn(q, kk, vv, dist, key_valid, sinks, rel_bias)
    return out, kk[:, -wb:], vv[:, -wb:]


def _gmlp_spatial(u, vn, w_s, b_s):
    N, L = u.shape[0], u.shape[1]
    Lp = -(-L // CHUNK) * CHUNK
    vc = jnp.pad(vn, ((0, 0), (0, Lp - L), (0, 0))).reshape(N, Lp // CHUNK, CHUNK, N_HEADS_GMLP, HEAD_DIM)
    w = w_s * jnp.tril(jnp.ones((CHUNK, CHUNK), w_s.dtype))
    mixed = jnp.einsum('hts,ncshd->ncthd', w, vc) + b_s.T[:, :, None]
    mixed = mixed.reshape(N, Lp, GMLP_WIDTH)[:, :L]
    return u * mixed


def _conv_ffn(h, prev, w_up, conv_w, conv_b, w_down):
    up = h @ w_up
    L = up.shape[1]
    ext = jnp.concatenate([prev.astype(up.dtype), up], axis=1)
    conv = conv_b + sum(conv_w[i] * ext[:, i:i + L] for i in range(CONV_WIDTH))
    gate, val = jnp.split(conv, 2, axis=-1)
    y = (jax.nn.gelu(gate, approximate=True) * val) @ w_down
    return y, ext[:, -(CONV_WIDTH - 1):]


def _layer(x, win_k, win_v, conv_prev, rel_bias, w_in, b_in, sinks, ln_g, ln_b, w_s, b_s,
           g_attn_out, g_gmlp_out, w_out, g_pre_mix, g_post_mix, g_pre_ffn, g_post_ffn,
           w_up, conv_w, conv_b, w_down):
    N, L = x.shape[0], x.shape[1]
    xn = _rms_norm(x, g_pre_mix)
    proj = xn @ w_in + b_in
    q, k, v, u, vg = jnp.split(proj, SPLITS, axis=-1)
    q = q.reshape(N, L, N_HEADS_ATTN, HEAD_DIM)
    k = k.reshape(N, L, N_KV_HEADS, HEAD_DIM)
    v = v.reshape(N, L, N_KV_HEADS, HEAD_DIM)
    if win_k is None:
        attn = _attn_prompt(q, k, v, sinks, rel_bias)
        new_k, new_v = k[:, -WINDOW:], v[:, -WINDOW:]
        conv_prev = jnp.zeros((N, CONV_WIDTH - 1, 2 * D_FF), x.dtype)
    else:
        attn, new_k, new_v = _attn_sample(q, k, v, win_k, win_v, sinks, rel_bias)
    u = jax.nn.gelu(u, approximate=False)
    vn = _layer_norm(jax.nn.gelu(vg, approximate=False), ln_g, ln_b)
    gm = _gmlp_spatial(u, vn, w_s, b_s)
    mixed = jnp.concatenate([_rms_norm(attn, g_attn_out), _rms_norm(gm, g_gmlp_out)], axis=-1) @ w_out
    h = x + _rms_norm(mixed, g_post_mix)
    f, new_conv = _conv_ffn(_rms_norm(h, g_pre_ffn), conv_prev, w_up, conv_w, conv_b, w_down)
    y = h + _rms_norm(f, g_post_ffn)
    return y, new_k, new_v, vn, new_conv


def setup_inputs(seed: int = 0) -> dict:
    key = jax.random.key(seed)
    ks = jax.random.split(key, 32)

    def nrm(k, shape, scale):
        return jax.random.normal(k, shape, jnp.float32) * scale

    wb = min(WINDOW, PAST_LEN)
    f2 = 2 * D_FF
    return {
        "x_prompt": nrm(ks[0], (BATCH, SEQ, D_MODEL), 1.0),
        "x_sample": nrm(ks[1], (DEC_BATCH, DEC_SEQ, D_MODEL), 1.0),
        "cache_win_k": nrm(ks[2], (DEPTH, DEC_BATCH, wb, N_KV_HEADS, HEAD_DIM), 1.0),
        "cache_win_v": nrm(ks[3], (DEPTH, DEC_BATCH, wb, N_KV_HEADS, HEAD_DIM), 1.0),
        "state_ffn_conv": nrm(ks[4], (DEPTH, DEC_BATCH, CONV_WIDTH - 1, f2), 1.0),
        "rel_bias": nrm(ks[5], (NUM_BUCKETS, N_HEADS_ATTN), 0.5),
        "w_in": nrm(ks[6], (DEPTH, D_MODEL, IN_WIDTH), D_MODEL ** -0.5),
        "b_in": nrm(ks[7], (DEPTH, IN_WIDTH), 0.02),
        "attn_sinks": nrm(ks[8], (DEPTH, N_HEADS_ATTN), 1.0),
        "gmlp_ln_g": 1.0 + nrm(ks[9], (DEPTH, GMLP_WIDTH), 0.02),
        "gmlp_ln_b": nrm(ks[10], (DEPTH, GMLP_WIDTH), 0.02),
        "gmlp_w_s": nrm(ks[11], (DEPTH, N_HEADS_GMLP, CHUNK, CHUNK), CHUNK ** -0.5),
        "gmlp_b_s": 1.0 + nrm(ks[12], (DEPTH, N_HEADS_GMLP, CHUNK), 0.02),
        "g_attn_out": 1.0 + nrm(ks[13], (DEPTH, ATTN_WIDTH), 0.02),
        "g_gmlp_out": 1.0 + nrm(ks[14], (DEPTH, GMLP_WIDTH), 0.02),
        "w_out": nrm(ks[15], (DEPTH, MIX_WIDTH, D_MODEL), MIX_WIDTH ** -0.5),
        "g_pre_mix": 1.0 + nrm(ks[16], (DEPTH, D_MODEL), 0.02),
        "g_post_mix": 1.0 + nrm(ks[17], (DEPTH, D_MODEL), 0.02),
        "g_pre_ffn": 1.0 + nrm(ks[18], (DEPTH, D_MODEL), 0.02),
        "g_post_ffn": 1.0 + nrm(ks[19], (DEPTH, D_MODEL), 0.02),
        "w_up": nrm(ks[20], (DEPTH, D_MODEL, f2), D_MODEL ** -0.5),
        "ffn_conv_w": nrm(ks[21], (DEPTH, CONV_WIDTH, f2), CONV_WIDTH ** -0.5),
        "ffn_conv_b": nrm(ks[22], (DEPTH, f2), 0.02),
        "w_down": nrm(ks[23], (DEPTH, D_FF, D_MODEL), D_FF ** -0.5),
    }


def reference(x_prompt, x_sample, cache_win_k, cache_win_v, state_ffn_conv, rel_bias,
              w_in, b_in, attn_sinks, gmlp_ln_g, gmlp_ln_b, gmlp_w_s, gmlp_b_s,
              g_attn_out, g_gmlp_out, w_out, g_pre_mix, g_post_mix, g_pre_ffn, g_post_ffn,
              w_up, ffn_conv_w, ffn_conv_b, w_down):
    yp, ys = x_prompt, x_sample
    kp_l, vp_l, ks_l, vs_l, gv_l, cp_l, cs_l = [], [], [], [], [], [], []
    for l in range(DEPTH):
        lw = (w_in[l], b_in[l], attn_sinks[l], gmlp_ln_g[l], gmlp_ln_b[l], gmlp_w_s[l], gmlp_b_s[l],
              g_attn_out[l], g_gmlp_out[l], w_out[l], g_pre_mix[l], g_post_mix[l], g_pre_ffn[l],
              g_post_ffn[l], w_up[l], ffn_conv_w[l], ffn_conv_b[l], w_down[l])
        yp, kp, vp, _, cp = _layer(yp, None, None, None, rel_bias, *lw)
        ys, ksm, vsm, gvs, csm = _layer(ys, cache_win_k[l], cache_win_v[l], state_ffn_conv[l], rel_bias, *lw)
        kp_l.append(kp); vp_l.append(vp); cp_l.append(cp)
        ks_l.append(ksm); vs_l.append(vsm); gv_l.append(gvs); cs_l.append(csm)
    win_k_prompt = jnp.stack(kp_l)
    win_v_prompt = jnp.stack(vp_l)
    win_k_sample = jnp.stack(ks_l)
    win_v_sample = jnp.stack(vs_l)
    gmlp_v_sample = jnp.stack(gv_l)
    ffn_conv_prompt = jnp.stack(cp_l)
    ffn_conv_sample = jnp.stack(cs_l)
    return (yp, ys, win_k_prompt, win_v_prompt, win_k_sample, win_v_sample, gmlp_v_sample, ffn_conv_prompt, ffn_conv_sample)
```

```python
import functools
import math

import numpy as np
import jax
import jax.numpy as jnp
from jax import lax
from jax.experimental import pallas as pl
from jax.experimental.pallas import tpu as pltpu

D_MODEL = 1024
HEAD_DIM = 64
N_HEADS = 8
N_KV = 2
N_PAIRS = N_HEADS // 2
BLOCK = 128
ATTN_W = N_HEADS * HEAD_DIM
KV_W = N_KV * HEAD_DIM
GMLP_W = 512
IN_W = ATTN_W + 2 * KV_W + 2 * GMLP_W
QKV_W = ATTN_W + 2 * KV_W
D_FF = 2816
FF_CHUNK = 256
N_FF_CHUNKS = D_FF // FF_CHUNK
NUM_BUCKETS = 32
MAX_DISTANCE = 128
EPS = 1e-6
NEG_INF = -1e30
SQRT_HALF = np.sqrt(0.5).astype(np.float32)
SQRT_2_OVER_PI = np.sqrt(2 / np.pi).astype(np.float32)

LANES = 128
SUBLANES = 8
VMEM_LIMIT_BYTES = 56 * 1024 * 1024

PROMPT_TILE = 512
SAMPLE_GROUP = 16

F32 = jnp.float32
BF16 = jnp.bfloat16


def _t5_bucket_np(dist):
    n = np.maximum(dist, 0)
    max_exact = NUM_BUCKETS // 2
    nf = np.maximum(n, 1).astype(np.float32)
    large = max_exact + (np.log(nf / max_exact) / math.log(MAX_DISTANCE / max_exact)
                         * (NUM_BUCKETS - max_exact)).astype(np.int32)
    large = np.minimum(large, NUM_BUCKETS - 1)
    return np.where(n < max_exact, n, large).astype(np.int32)


def _prompt_bucket_tile():
    r = np.arange(BLOCK)[:, None]
    c = np.arange(2 * BLOCK)[None, :]
    dist = r + BLOCK - c
    ok = (dist >= 0) & (dist <= BLOCK)
    return np.where(ok, _t5_bucket_np(dist), -1).astype(np.int32)


def _sample_bucket_row():
    c = np.arange(2 * BLOCK)[None, :]
    dist = BLOCK - c
    ok = dist >= 0
    return np.where(ok, _t5_bucket_np(dist), -1).astype(np.int32)


def _rms(x, g):
    ms = jnp.mean(x * x, axis=-1, keepdims=True)
    return x * lax.rsqrt(ms + EPS) * g


def _layer_norm(x, g, b):
    mu = jnp.mean(x, axis=-1, keepdims=True)
    xc = x - mu
    y = xc * lax.rsqrt(jnp.mean(xc * xc, axis=-1, keepdims=True) + EPS)
    return y * g + b


def _gelu_erf(x):
    return 0.5 * x * (1.0 + lax.erf(x * SQRT_HALF))


def _gelu_tanh(x):
    cdf = 0.5 * (1.0 + jnp.tanh(SQRT_2_OVER_PI * (x + 0.044715 * (x * x * x))))
    return x * cdf


def _softmax_with_sink(s, sink):
    m = jnp.maximum(jnp.max(s, axis=-1, keepdims=True), sink)
    p = jnp.exp(s - m)
    l = jnp.sum(p, axis=-1, keepdims=True) + jnp.exp(sink - m)
    return p, l


def _dot(a, b):
    return jnp.dot(a, b, preferred_element_type=F32)


def _dot_nt(a, b):
    return lax.dot_general(a, b, (((1,), (1,)), ((), ())), preferred_element_type=F32)


def _mixer_kernel(x_ref, win_ref, bin_ref, relb_ref, sink_ref, bucket_ref, lng_ref, lnb_ref,
                  ws_ref, bs_ref, gatt_ref, ggm_ref, wout_ref, gpre_ref, gpost_ref,
                  h_ref, wk_ref, wv_ref,
                  bias_s, q_s, kv_s, u_s, vnp_s, wsp_s, mix_s, *, tile, n_tiles):
    b = pl.program_id(0)
    t = pl.program_id(1)
    n_blk = tile // BLOCK

    @pl.when((b == 0) & (t == 0))
    def _init():
        bt = bucket_ref[...]
        for h in range(N_HEADS):
            acc = jnp.zeros((BLOCK, 2 * BLOCK), F32)
            for bk in range(NUM_BUCKETS):
                acc = jnp.where(bt == bk, relb_ref[bk, h], acc)
            bias_s[h] = acc
        row = lax.broadcasted_iota(jnp.int32, (BLOCK, BLOCK), 0)
        col = lax.broadcasted_iota(jnp.int32, (BLOCK, BLOCK), 1)
        tril = (row >= col).astype(F32)
        for p in range(N_PAIRS):
            wsp_s[p] = jnp.concatenate(
                [ws_ref[2 * p] * tril, ws_ref[2 * p + 1] * tril], axis=1).astype(BF16)

    @pl.when(t == 0)
    def _zero_prev():
        kv_s[:, 0:BLOCK, :] = jnp.zeros((8, BLOCK, LANES), BF16)

    @pl.when(t > 0)
    def _carry_prev():
        kv_s[:, 0:BLOCK, :] = kv_s[:, tile:tile + BLOCK, :]

    x = x_ref[0]
    xn = _rms(x, gpre_ref[...])
    proj = _dot(xn.astype(BF16), win_ref[...]) + bin_ref[...]

    q_s[...] = (proj[:, :ATTN_W] * (HEAD_DIM ** -0.5)).astype(BF16)
    k = proj[:, ATTN_W:ATTN_W + KV_W]
    v = proj[:, ATTN_W + KV_W:QKV_W]

    @pl.when(t == n_tiles - 1)
    def _emit_window():
        wk_ref[0] = k[tile - BLOCK:, :]
        wv_ref[0] = v[tile - BLOCK:, :]

    lo = lax.broadcasted_iota(jnp.int32, (tile, LANES), 1) < HEAD_DIM
    for base, val in ((0, k), (4, v)):
        rolled = pltpu.roll(val, HEAD_DIM, axis=1)
        kv_s[base + 0, BLOCK:, :] = jnp.where(lo, val, 0.0).astype(BF16)
        kv_s[base + 1, BLOCK:, :] = jnp.where(lo, 0.0, rolled).astype(BF16)
        kv_s[base + 2, BLOCK:, :] = jnp.where(lo, rolled, 0.0).astype(BF16)
        kv_s[base + 3, BLOCK:, :] = jnp.where(lo, 0.0, val).astype(BF16)

    u_s[...] = _gelu_erf(proj[:, QKV_W:QKV_W + GMLP_W])
    vn = _layer_norm(_gelu_erf(proj[:, QKV_W + GMLP_W:]), lng_ref[...], lnb_ref[...])
    lo4 = (lax.broadcasted_iota(jnp.int32, (tile, GMLP_W), 1) & (LANES - 1)) < HEAD_DIM
    vnp_s[0] = jnp.where(lo4, vn, 0.0).astype(BF16)
    vnp_s[1] = jnp.where(lo4, 0.0, vn).astype(BF16)

    def block_body(j, carry):
        r0 = pl.multiple_of(j * BLOCK, BLOCK)
        rows = pl.ds(r0, BLOCK)
        krows = pl.ds(r0, 2 * BLOCK)
        r = lax.broadcasted_iota(jnp.int32, (BLOCK, 2 * BLOCK), 0)
        c = lax.broadcasted_iota(jnp.int32, (BLOCK, 2 * BLOCK), 1)
        dist = r + BLOCK - c
        first_col = jnp.where((t * n_blk + j) > 0, 0, BLOCK)
        allowed = (dist >= 0) & (dist <= BLOCK) & (c >= first_col)
        lane_lo = lax.broadcasted_iota(jnp.int32, (BLOCK, LANES), 1) < HEAD_DIM

        attn_parts = []
        gm_parts = []
        for p in range(N_PAIRS):
            g = p // 2
            cols = slice(p * LANES, (p + 1) * LANES)
            qp = q_s[rows, cols]
            parts = []
            for par in range(2):
                h = 2 * p + par
                s = _dot_nt(qp, kv_s[2 * g + par, krows, :])
                s = jnp.where(allowed, s + bias_s[h], NEG_INF)
                parts.append(_softmax_with_sink(s, sink_ref[h]))
            (p_e, l_e), (p_o, l_o) = parts
            pmat = jnp.concatenate([p_e.astype(BF16), p_o.astype(BF16)], axis=1)
            vmat = jnp.concatenate([kv_s[4 + 2 * g, krows, :], kv_s[4 + 2 * g + 1, krows, :]], axis=0)
            o = _dot(pmat, vmat)
            attn_parts.append(o * jnp.where(lane_lo, 1.0 / l_e, 1.0 / l_o))

            vst = jnp.concatenate([vnp_s[0, rows, cols], vnp_s[1, rows, cols]], axis=0)
            mixed = _dot(wsp_s[p], vst) + bs_ref[:, cols]
            gm_parts.append(u_s[rows, cols] * mixed)

        attn = jnp.concatenate(attn_parts, axis=1)
        gm = jnp.concatenate(gm_parts, axis=1)
        mix_s[rows, 0:ATTN_W] = _rms(attn, gatt_ref[...]).astype(BF16)
        mix_s[rows, ATTN_W:] = _rms(gm, ggm_ref[...]).astype(BF16)
        return carry

    lax.fori_loop(0, n_blk, block_body, 0)

    mixed = _dot(mix_s[...], wout_ref[...])
    h_ref[0] = x + _rms(mixed, gpost_ref[...])


def _const_spec(shape):
    nd = len(shape)
    return pl.BlockSpec(shape, lambda *_: (0,) * nd, pipeline_mode=pl.Buffered(1))


def _smem_spec():
    return pl.BlockSpec(memory_space=pltpu.SMEM)


def _prompt_mixer(x, w_in, b_in, rel_bias, sinks, ln_g, ln_b, w_s, bs_full, g_att, g_gm, w_out,
                  g_pre, g_post):
    B, L, _ = x.shape
    tile = PROMPT_TILE
    n_tiles = L // tile
    bucket = jnp.asarray(_prompt_bucket_tile())
    kern = functools.partial(_mixer_kernel, tile=tile, n_tiles=n_tiles)
    return pl.pallas_call(
        kern,
        grid=(B, n_tiles),
        in_specs=[
            pl.BlockSpec((1, tile, D_MODEL), lambda b, t: (b, t, 0)),
            _const_spec((D_MODEL, IN_W)),
            _const_spec((1, IN_W)),
            _smem_spec(),
            _smem_spec(),
            _const_spec((BLOCK, 2 * BLOCK)),
            _const_spec((1, GMLP_W)),
            _const_spec((1, GMLP_W)),
            _const_spec((N_HEADS, BLOCK, BLOCK)),
            _const_spec((BLOCK, GMLP_W)),
            _const_spec((1, ATTN_W)),
            _const_spec((1, GMLP_W)),
            _const_spec((D_MODEL, D_MODEL)),
            _const_spec((1, D_MODEL)),
            _const_spec((1, D_MODEL)),
        ],
        out_specs=[
            pl.BlockSpec((1, tile, D_MODEL), lambda b, t: (b, t, 0)),
            pl.BlockSpec((1, BLOCK, KV_W), lambda b, t: (b, 0, 0)),
            pl.BlockSpec((1, BLOCK, KV_W), lambda b, t: (b, 0, 0)),
        ],
        out_shape=[
            jax.ShapeDtypeStruct((B, L, D_MODEL), F32),
            jax.ShapeDtypeStruct((B, BLOCK, KV_W), F32),
            jax.ShapeDtypeStruct((B, BLOCK, KV_W), F32),
        ],
        scratch_shapes=[
            pltpu.VMEM((N_HEADS, BLOCK, 2 * BLOCK), F32),
            pltpu.VMEM((tile, ATTN_W), BF16),
            pltpu.VMEM((8, tile + BLOCK, LANES), BF16),
            pltpu.VMEM((tile, GMLP_W), F32),
            pltpu.VMEM((2, tile, GMLP_W), BF16),
            pltpu.VMEM((N_PAIRS, BLOCK, 2 * BLOCK), BF16),
            pltpu.VMEM((tile, D_MODEL), BF16),
        ],
        compiler_params=pltpu.CompilerParams(
            dimension_semantics=("arbitrary", "arbitrary"),
            vmem_limit_bytes=VMEM_LIMIT_BYTES),
        name="prompt_mixer",
    )(x, w_in, b_in, rel_bias, sinks, bucket, ln_g, ln_b, w_s, bs_full, g_att, g_gm, w_out,
      g_pre, g_post)


def _shift_rows(u, prev8, n):
    s = pltpu.roll(u, n, axis=0)
    top = jnp.where(lax.broadcasted_iota(jnp.int32, prev8.shape, 0) < n,
                    pltpu.roll(prev8, n, axis=0), s[0:SUBLANES])
    return jnp.concatenate([top, s[SUBLANES:]], axis=0)


def _ffn_kernel(h_ref, wup_ref, cw_ref, wdown_ref, gpre_ref, gpost_ref,
                y_ref, cst_ref, h2_s, acc_s, carry_s, *, tile):
    t = pl.program_id(1)

    @pl.when(t == 0)
    def _zero_state():
        carry_s[...] = jnp.zeros(carry_s.shape, F32)

    h = h_ref[0]
    h2_s[...] = _rms(h, gpre_ref[...]).astype(BF16)
    acc_s[...] = jnp.zeros((tile, D_MODEL), F32)

    def chunk_body(c, carry):
        h2 = h2_s[...]
        conv = []
        for gv in range(2):
            u = _dot(h2, wup_ref[gv, c])
            prev8 = carry_s[gv, c]
            cw = cw_ref[gv, c]
            conv.append(cw[3:4] + cw[0:1] * _shift_rows(u, prev8, 2)
                        + cw[1:2] * _shift_rows(u, prev8, 1) + cw[2:3] * u)
            carry_s[gv, c] = u[tile - SUBLANES:, :]
        act = (_gelu_tanh(conv[0]) * conv[1]).astype(BF16)
        acc_s[...] += _dot(act, wdown_ref[c])
        return carry

    lax.fori_loop(0, N_FF_CHUNKS, chunk_body, 0)
    cst_ref[0] = carry_s[...]
    y_ref[0] = h + _rms(acc_s[...], gpost_ref[...])


def _prompt_ffn(h, wup_c, cw_c, wdown_c, g_pre, g_post):
    B, L, _ = h.shape
    tile = PROMPT_TILE
    n_tiles = L // tile
    kern = functools.partial(_ffn_kernel, tile=tile)
    return pl.pallas_call(
        kern,
        grid=(B, n_tiles),
        in_specs=[
            pl.BlockSpec((1, tile, D_MODEL), lambda b, t: (b, t, 0)),
            _const_spec((2, N_FF_CHUNKS, D_MODEL, FF_CHUNK)),
            _const_spec((2, N_FF_CHUNKS, 4, FF_CHUNK)),
            _const_spec((N_FF_CHUNKS, FF_CHUNK, D_MODEL)),
            _const_spec((1, D_MODEL)),
            _const_spec((1, D_MODEL)),
        ],
        out_specs=[
            pl.BlockSpec((1, tile, D_MODEL), lambda b, t: (b, t, 0)),
            pl.BlockSpec((1, 2, N_FF_CHUNKS, SUBLANES, FF_CHUNK), lambda b, t: (b, 0, 0, 0, 0)),
        ],
        out_shape=[
            jax.ShapeDtypeStruct((B, L, D_MODEL), F32),
            jax.ShapeDtypeStruct((B, 2, N_FF_CHUNKS, SUBLANES, FF_CHUNK), F32),
        ],
        scratch_shapes=[
            pltpu.VMEM((tile, D_MODEL), BF16),
            pltpu.VMEM((tile, D_MODEL), F32),
            pltpu.VMEM((2, N_FF_CHUNKS, SUBLANES, FF_CHUNK), F32),
        ],
        compiler_params=pltpu.CompilerParams(
            dimension_semantics=("arbitrary", "arbitrary"),
            vmem_limit_bytes=VMEM_LIMIT_BYTES),
        name="prompt_ffn",
    )(h, wup_c, cw_c, wdown_c, g_pre, g_post)


def _sample_attn_kernel(x_ref, wqkv_ref, bqkv_ref, gpre_ref, ck_ref, cv_ref, relbt_ref, sink_ref,
                        bucket_ref, attn_ref, wk_ref, wv_ref,
                        bias_s, kext_s, vext_s, rep_s, *, group):
    i = pl.program_id(0)
    rows = group * SUBLANES

    @pl.when(i == 0)
    def _init():
        bt = jnp.broadcast_to(bucket_ref[...], (N_HEADS, 2 * BLOCK))
        acc = jnp.zeros((N_HEADS, 2 * BLOCK), F32)
        for bk in range(NUM_BUCKETS):
            acc = jnp.where(bt == bk, relbt_ref[:, bk:bk + 1], acc)
        bias_s[...] = acc
        kext_s[...] = jnp.zeros(kext_s.shape, F32)
        vext_s[...] = jnp.zeros(vext_s.shape, F32)

    xn = _rms(x_ref[...], gpre_ref[...])
    xr = jnp.concatenate(
        [jnp.broadcast_to(xn[s:s + 1, :], (SUBLANES, D_MODEL)) for s in range(group)], axis=0)
    proj = _dot(xr.astype(BF16), wqkv_ref[...]) + bqkv_ref[...]
    q = proj[:, :ATTN_W] * (HEAD_DIM ** -0.5)
    k3 = proj[:, ATTN_W:ATTN_W + KV_W].reshape(group, SUBLANES, LANES)
    v3 = proj[:, ATTN_W + KV_W:].reshape(group, SUBLANES, LANES)

    head = lax.broadcasted_iota(jnp.int32, (rows, LANES), 0) & (SUBLANES - 1)
    lane_half = lax.broadcasted_iota(jnp.int32, (rows, LANES), 1) // HEAD_DIM
    lane_lo = lane_half == 0
    qpad = jnp.zeros((rows, LANES), F32)
    for p in range(N_PAIRS):
        g = p // 2
        qp = q[:, p * LANES:(p + 1) * LANES]
        qr = pltpu.roll(qp, HEAD_DIM, axis=1)
        src_even = qp if g == 0 else qr
        src_odd = qr if g == 0 else qp
        qpad = jnp.where((head == 2 * p) & (lane_half == g), src_even, qpad)
        qpad = jnp.where((head == 2 * p + 1) & (lane_half == g), src_odd, qpad)
    qpad = qpad.reshape(group, SUBLANES, LANES).astype(BF16)

    ck = ck_ref[...]
    cv = cv_ref[...]
    kext_s[:, 0:BLOCK, :] = ck
    vext_s[:, 0:BLOCK, :] = cv
    kext_s[:, BLOCK:BLOCK + SUBLANES, :] = k3
    vext_s[:, BLOCK:BLOCK + SUBLANES, :] = v3

    s = jnp.einsum('shc,sjc->shj', qpad, kext_s[...].astype(BF16), preferred_element_type=F32)
    allowed = jnp.broadcast_to(bucket_ref[...], (N_HEADS, 2 * BLOCK)) >= 0
    s = jnp.where(allowed[None], s + bias_s[...][None], NEG_INF)
    pr, l = _softmax_with_sink(s, sink_ref[...][None])
    o = jnp.einsum('shj,sjc->shc', pr.astype(BF16), vext_s[...].astype(BF16),
                   preferred_element_type=F32)
    o = (o * (1.0 / l)).reshape(rows, LANES)

    om = jnp.where(lane_half == head // (N_HEADS // N_KV), o, 0.0)
    both = (om + pltpu.roll(om, HEAD_DIM, axis=1)).reshape(group, SUBLANES, LANES)
    lane_lo3 = lane_lo.reshape(group, SUBLANES, LANES)
    for p in range(N_PAIRS):
        even = jnp.broadcast_to(both[:, 2 * p:2 * p + 1, :], (group, SUBLANES, LANES))
        odd = jnp.broadcast_to(both[:, 2 * p + 1:2 * p + 2, :], (group, SUBLANES, LANES))
        rep_s[p] = jnp.where(lane_lo3, even, odd).reshape(rows, LANES)
    for p in range(N_PAIRS):
        attn_ref[:, p * LANES:(p + 1) * LANES] = rep_s[p, pl.ds(0, group, stride=SUBLANES), :]

    last = lax.broadcasted_iota(jnp.int32, (group, SUBLANES, LANES), 1) == SUBLANES - 1
    for cache, new3, out_ref in ((ck, k3, wk_ref), (cv, v3, wv_ref)):
        shifted = pltpu.roll(cache, BLOCK - 1, axis=1)
        out_ref[:, 0:BLOCK - SUBLANES, :] = shifted[:, 0:BLOCK - SUBLANES, :]
        out_ref[:, BLOCK - SUBLANES:, :] = jnp.where(last, new3, shifted[:, BLOCK - SUBLANES:, :])


def _sample_attention(x, wqkv, bqkv, g_pre, cache_k, cache_v, relb_t, sinks_col):
    S = x.shape[0]
    group = SAMPLE_GROUP
    bucket = jnp.asarray(_sample_bucket_row())
    kern = functools.partial(_sample_attn_kernel, group=group)
    return pl.pallas_call(
        kern,
        grid=(S // group,),
        in_specs=[
            pl.BlockSpec((group, D_MODEL), lambda i: (i, 0)),
            _const_spec((D_MODEL, QKV_W)),
            _const_spec((1, QKV_W)),
            _const_spec((1, D_MODEL)),
            pl.BlockSpec((group, BLOCK, KV_W), lambda i: (i, 0, 0)),
            pl.BlockSpec((group, BLOCK, KV_W), lambda i: (i, 0, 0)),
            _const_spec((N_HEADS, NUM_BUCKETS)),
            _const_spec((N_HEADS, 1)),
            _const_spec((1, 2 * BLOCK)),
        ],
        out_specs=[
            pl.BlockSpec((group, ATTN_W), lambda i: (i, 0)),
            pl.BlockSpec((group, BLOCK, KV_W), lambda i: (i, 0, 0)),
            pl.BlockSpec((group, BLOCK, KV_W), lambda i: (i, 0, 0)),
        ],
        out_shape=[
            jax.ShapeDtypeStruct((S, ATTN_W), F32),
            jax.ShapeDtypeStruct((S, BLOCK, KV_W), F32),
            jax.ShapeDtypeStruct((S, BLOCK, KV_W), F32),
        ],
        scratch_shapes=[
            pltpu.VMEM((N_HEADS, 2 * BLOCK), F32),
            pltpu.VMEM((group, 2 * BLOCK, LANES), F32),
            pltpu.VMEM((group, 2 * BLOCK, LANES), F32),
            pltpu.VMEM((N_PAIRS, group * SUBLANES, LANES), F32),
        ],
        compiler_params=pltpu.CompilerParams(
            dimension_semantics=("arbitrary",),
            vmem_limit_bytes=VMEM_LIMIT_BYTES),
        name="sample_attention",
    )(x, wqkv, bqkv, g_pre, cache_k, cache_v, relb_t, sinks_col, bucket)


def _sample_tail_kernel(x_ref, attn_ref, st_ref, wuv_ref, buv_ref, lng_ref, lnb_ref, ws0_ref, bs0_ref,
                        gatt_ref, ggm_ref, wout_ref, gpre_ref, gpost_ref, gpre2_ref, gpost2_ref,
                        wup_ref, cw_ref, wdown_ref,
                        y_ref, vn_ref, nst_ref):
    x = x_ref[...]
    xn = _rms(x, gpre_ref[...])
    puv = _dot(xn.astype(BF16), wuv_ref[...]) + buv_ref[...]
    u = _gelu_erf(puv[:, :GMLP_W])
    vn = _layer_norm(_gelu_erf(puv[:, GMLP_W:]), lng_ref[...], lnb_ref[...])
    vn_ref[...] = vn
    gm = u * (ws0_ref[...] * vn + bs0_ref[...])
    mix = jnp.concatenate([_rms(attn_ref[...], gatt_ref[...]), _rms(gm, ggm_ref[...])], axis=1)
    h = x + _rms(_dot(mix.astype(BF16), wout_ref[...]), gpost_ref[...])

    h2 = _rms(h, gpre2_ref[...]).astype(BF16)
    two_ff = 2 * D_FF
    nst_ref[:, 0:two_ff] = st_ref[:, two_ff:]
    acc = jnp.zeros((x.shape[0], D_MODEL), F32)
    for c in range(N_FF_CHUNKS):
        conv = []
        for gv in range(2):
            col = gv * D_FF + c * FF_CHUNK
            u_c = _dot(h2, wup_ref[gv, c])
            cw = cw_ref[gv, c]
            conv.append(cw[3:4] + cw[0:1] * st_ref[:, col:col + FF_CHUNK]
                        + cw[1:2] * st_ref[:, two_ff + col:two_ff + col + FF_CHUNK] + cw[2:3] * u_c)
            nst_ref[:, two_ff + col:two_ff + col + FF_CHUNK] = u_c
        act = (_gelu_tanh(conv[0]) * conv[1]).astype(BF16)
        acc = acc + _dot(act, wdown_ref[c])
    y_ref[...] = h + _rms(acc, gpost2_ref[...])


def _sample_tail(x, attn, state, wuv, buv, ln_g, ln_b, ws0, bs0, g_att, g_gm, w_out, g_pre, g_post,
                 g_pre2, g_post2, wup_c, cw_c, wdown_c):
    S = x.shape[0]
    args = (x, attn, state, wuv, buv, ln_g, ln_b, ws0, bs0, g_att, g_gm, w_out, g_pre, g_post,
            g_pre2, g_post2, wup_c, cw_c, wdown_c)
    return pl.pallas_call(
        _sample_tail_kernel,
        grid=(1,),
        in_specs=[_const_spec(a.shape) for a in args],
        out_specs=[
            pl.BlockSpec((S, D_MODEL), lambda i: (0, 0)),
            pl.BlockSpec((S, GMLP_W), lambda i: (0, 0)),
            pl.BlockSpec((S, 4 * D_FF), lambda i: (0, 0)),
        ],
        out_shape=[
            jax.ShapeDtypeStruct((S, D_MODEL), F32),
            jax.ShapeDtypeStruct((S, GMLP_W), F32),
            jax.ShapeDtypeStruct((S, 4 * D_FF), F32),
        ],
        compiler_params=pltpu.CompilerParams(
            dimension_semantics=("arbitrary",),
            vmem_limit_bytes=VMEM_LIMIT_BYTES),
        name="sample_tail",
    )(*args)


def kernel(x_prompt, x_sample, cache_win_k, cache_win_v, state_ffn_conv, rel_bias, w_in, b_in, attn_sinks, gmlp_ln_g, gmlp_ln_b, gmlp_w_s, gmlp_b_s, g_attn_out, g_gmlp_out, w_out, g_pre_mix, g_post_mix, g_pre_ffn, g_post_ffn, w_up, ffn_conv_w, ffn_conv_b, w_down):
    depth = w_in.shape[0]
    assert depth == 1, "single-layer step"
    B, L, _ = x_prompt.shape
    S = x_sample.shape[0]
    assert x_sample.shape[1] == 1 and L % PROMPT_TILE == 0 and S % SAMPLE_GROUP == 0
    assert cache_win_k.shape[2] == BLOCK

    row = lambda a: a.reshape(1, -1)
    w_in_b = w_in[0].astype(BF16)
    w_out_b = w_out[0].astype(BF16)
    b_in_r = row(b_in[0])
    wup_c = w_up[0].astype(BF16).reshape(D_MODEL, 2, N_FF_CHUNKS, FF_CHUNK).transpose(1, 2, 0, 3)
    wdown_c = w_down[0].astype(BF16).reshape(N_FF_CHUNKS, FF_CHUNK, D_MODEL)
    cw_c = jnp.concatenate([ffn_conv_w[0], ffn_conv_b[0][None]], axis=0)
    cw_c = cw_c.reshape(4, 2, N_FF_CHUNKS, FF_CHUNK).transpose(1, 2, 0, 3)
    bs_full = jnp.repeat(gmlp_b_s[0].T, HEAD_DIM, axis=1)
    ln_g, ln_b = row(gmlp_ln_g[0]), row(gmlp_ln_b[0])
    g_att, g_gm = row(g_attn_out[0]), row(g_gmlp_out[0])
    g_pre, g_post = row(g_pre_mix[0]), row(g_post_mix[0])
    g_pre2, g_post2 = row(g_pre_ffn[0]), row(g_post_ffn[0])

    h_p, wk_p, wv_p = _prompt_mixer(x_prompt, w_in_b, b_in_r, rel_bias, attn_sinks[0], ln_g, ln_b,
                                    gmlp_w_s[0], bs_full, g_att, g_gm, w_out_b, g_pre, g_post)
    y_p, cst_p = _prompt_ffn(h_p, wup_c, cw_c, wdown_c, g_pre2, g_post2)
    conv_p = cst_p[:, :, :, SUBLANES - 2:, :].transpose(0, 3, 1, 2, 4).reshape(1, B, 2, 2 * D_FF)

    xs = x_sample.reshape(S, D_MODEL)
    attn_s, wk_s, wv_s = _sample_attention(
        xs, w_in_b[:, :QKV_W], b_in_r[:, :QKV_W], g_pre,
        cache_win_k[0].reshape(S, BLOCK, KV_W), cache_win_v[0].reshape(S, BLOCK, KV_W),
        rel_bias.T, attn_sinks[0].reshape(N_HEADS, 1))
    ws0 = row(jnp.repeat(gmlp_w_s[0][:, 0, 0], HEAD_DIM))
    bs0 = row(jnp.repeat(gmlp_b_s[0][:, 0], HEAD_DIM))
    y_s, vn_s, nst_s = _sample_tail(
        xs, attn_s, state_ffn_conv[0].reshape(S, 4 * D_FF), w_in_b[:, QKV_W:], b_in_r[:, QKV_W:],
        ln_g, ln_b, ws0, bs0, g_att, g_gm, w_out_b, g_pre, g_post, g_pre2, g_post2,
        wup_c, cw_c, wdown_c)

    kv_shape_p = (1, B, BLOCK, N_KV, HEAD_DIM)
    kv_shape_s = (1, S, BLOCK, N_KV, HEAD_DIM)
    return (y_p,
            y_s.reshape(S, 1, D_MODEL),
            wk_p.reshape(kv_shape_p), wv_p.reshape(kv_shape_p),
            wk_s.reshape(kv_shape_s), wv_s.reshape(kv_shape_s),
            vn_s.reshape(1, S, 1, GMLP_W),
            conv_p,
            nst_s.reshape(1, S, 2, 2 * D_FF))
```

```python
import functools
import math

import numpy as np
import jax
import jax.numpy as jnp
from jax import lax
from jax.experimental import pallas as pl
from jax.experimental.pallas import tpu as pltpu

D_MODEL = 1024
HEAD_DIM = 64
N_HEADS = 8
N_KV = 2
N_PAIRS = N_HEADS // 2
BLOCK = 128
ATTN_W = N_HEADS * HEAD_DIM
KV_W = N_KV * HEAD_DIM
GMLP_W = 512
IN_W = ATTN_W + 2 * KV_W + 2 * GMLP_W
QKV_W = ATTN_W + 2 * KV_W
D_FF = 2816
FF_CHUNK = 256
N_FF_CHUNKS = D_FF // FF_CHUNK
NUM_BUCKETS = 32
MAX_DISTANCE = 128
EPS = 1e-6
NEG_INF = -1e30
SQRT_HALF = np.sqrt(0.5).astype(np.float32)
SQRT_2_OVER_PI = np.sqrt(2 / np.pi).astype(np.float32)

LANES = 128
SUBLANES = 8
VMEM_LIMIT_BYTES = 56 * 1024 * 1024

PROMPT_TILE = 512
SAMPLE_GROUP = 16

F32 = jnp.float32
BF16 = jnp.bfloat16


def _t5_bucket_np(dist):
    n = np.maximum(dist, 0)
    max_exact = NUM_BUCKETS // 2
    nf = np.maximum(n, 1).astype(np.float32)
    large = max_exact + (np.log(nf / max_exact) / math.log(MAX_DISTANCE / max_exact)
                         * (NUM_BUCKETS - max_exact)).astype(np.int32)
    large = np.minimum(large, NUM_BUCKETS - 1)
    return np.where(n < max_exact, n, large).astype(np.int32)


def _prompt_bucket_tile():
    r = np.arange(BLOCK)[:, None]
    c = np.arange(2 * BLOCK)[None, :]
    dist = r + BLOCK - c
    ok = (dist >= 0) & (dist <= BLOCK)
    return np.where(ok, _t5_bucket_np(dist), -1).astype(np.int32)


def _sample_bucket_row():
    c = np.arange(2 * BLOCK)[None, :]
    dist = BLOCK - c
    ok = dist >= 0
    return np.where(ok, _t5_bucket_np(dist), -1).astype(np.int32)


def _rms(x, g):
    ms = jnp.mean(x * x, axis=-1, keepdims=True)
    return x * lax.rsqrt(ms + EPS) * g


def _layer_norm(x, g, b):
    mu = jnp.mean(x, axis=-1, keepdims=True)
    xc = x - mu
    y = xc * lax.rsqrt(jnp.mean(xc * xc, axis=-1, keepdims=True) + EPS)
    return y * g + b


def _gelu_erf(x):
    return 0.5 * x * (1.0 + lax.erf(x * SQRT_HALF))


def _gelu_tanh(x):
    cdf = 0.5 * (1.0 + jnp.tanh(SQRT_2_OVER_PI * (x + 0.044715 * (x * x * x))))
    return x * cdf


def _softmax_with_sink(s, sink):
    m = jnp.maximum(jnp.max(s, axis=-1, keepdims=True), sink)
    p = jnp.exp(s - m)
    l = jnp.sum(p, axis=-1, keepdims=True) + jnp.exp(sink - m)
    return p, l


def _dot(a, b):
    return jnp.dot(a, b, preferred_element_type=F32)


def _dot_nt(a, b):
    return lax.dot_general(a, b, (((1,), (1,)), ((), ())), preferred_element_type=F32)


def _mixer_kernel(x_ref, win_ref, bin_ref, relb_ref, sink_ref, bucket_ref, lng_ref, lnb_ref,
                  ws_ref, bs_ref, gatt_ref, ggm_ref, wout_ref, gpre_ref, gpost_ref,
                  h_ref, wk_ref, wv_ref,
                  bias_s, q_s, kv_s, u_s, vnp_s, wsp_s, mix_s, *, tile, n_tiles):
    b = pl.program_id(0)
    t = pl.program_id(1)
    n_blk = tile // BLOCK

    @pl.when((b == 0) & (t == 0))
    def _init():
        bt = bucket_ref[...]
        for h in range(N_HEADS):
            acc = jnp.zeros((BLOCK, 2 * BLOCK), F32)
            for bk in range(NUM_BUCKETS):
                acc = jnp.where(bt == bk, relb_ref[bk, h], acc)
            bias_s[h] = acc
        row = lax.broadcasted_iota(jnp.int32, (BLOCK, BLOCK), 0)
        col = lax.broadcasted_iota(jnp.int32, (BLOCK, BLOCK), 1)
        tril = (row >= col).astype(F32)
        for p in range(N_PAIRS):
            wsp_s[p] = jnp.concatenate(
                [ws_ref[2 * p] * tril, ws_ref[2 * p + 1] * tril], axis=1).astype(BF16)

    @pl.when(t == 0)
    def _zero_prev():
        kv_s[:, 0:BLOCK, :] = jnp.zeros((8, BLOCK, LANES), BF16)

    @pl.when(t > 0)
    def _carry_prev():
        kv_s[:, 0:BLOCK, :] = kv_s[:, tile:tile + BLOCK, :]

    x = x_ref[0]
    xn = _rms(x, gpre_ref[...])
    proj = _dot(xn.astype(BF16), win_ref[...]) + bin_ref[...]

    q_s[...] = (proj[:, :ATTN_W] * (HEAD_DIM ** -0.5)).astype(BF16)
    k = proj[:, ATTN_W:ATTN_W + KV_W]
    v = proj[:, ATTN_W + KV_W:QKV_W]

    @pl.when(t == n_tiles - 1)
    def _emit_window():
        wk_ref[0] = k[tile - BLOCK:, :]
        wv_ref[0] = v[tile - BLOCK:, :]

    lo = lax.broadcasted_iota(jnp.int32, (tile, LANES), 1) < HEAD_DIM
    for base, val in ((0, k), (4, v)):
        rolled = pltpu.roll(val, HEAD_DIM, axis=1)
        kv_s[base + 0, BLOCK:, :] = jnp.where(lo, val, 0.0).astype(BF16)
        kv_s[base + 1, BLOCK:, :] = jnp.where(lo, 0.0, rolled).astype(BF16)
        kv_s[base + 2, BLOCK:, :] = jnp.where(lo, rolled, 0.0).astype(BF16)
        kv_s[base + 3, BLOCK:, :] = jnp.where(lo, 0.0, val).astype(BF16)

    u_s[...] = _gelu_erf(proj[:, QKV_W:QKV_W + GMLP_W])
    vn = _layer_norm(_gelu_erf(proj[:, QKV_W + GMLP_W:]), lng_ref[...], lnb_ref[...])
    lo4 = (lax.broadcasted_iota(jnp.int32, (tile, GMLP_W), 1) & (LANES - 1)) < HEAD_DIM
    vnp_s[0] = jnp.where(lo4, vn, 0.0).astype(BF16)
    vnp_s[1] = jnp.where(lo4, 0.0, vn).astype(BF16)

    r = lax.broadcasted_iota(jnp.int32, (BLOCK, 2 * BLOCK), 0)
    c = lax.broadcasted_iota(jnp.int32, (BLOCK, 2 * BLOCK), 1)
    dist = r + BLOCK - c
    in_band = (dist >= 0) & (dist <= BLOCK)
    lane_lo = lax.broadcasted_iota(jnp.int32, (BLOCK, LANES), 1) < HEAD_DIM

    for j in range(n_blk):
        rows = slice(j * BLOCK, (j + 1) * BLOCK)
        krows = slice(j * BLOCK, (j + 2) * BLOCK)
        if j == 0:
            allowed = in_band & (c >= jnp.where(t > 0, 0, BLOCK))
        else:
            allowed = in_band

        attn_parts = []
        gm_parts = []
        for p in range(N_PAIRS):
            g = p // 2
            cols = slice(p * LANES, (p + 1) * LANES)
            qp = q_s[rows, cols]
            parts = []
            for par in range(2):
                h = 2 * p + par
                s = _dot_nt(qp, kv_s[2 * g + par, krows, :])
                s = jnp.where(allowed, s + bias_s[h], NEG_INF)
                parts.append(_softmax_with_sink(s, sink_ref[h]))
            (p_e, l_e), (p_o, l_o) = parts
            pmat = jnp.concatenate([p_e.astype(BF16), p_o.astype(BF16)], axis=1)
            vmat = jnp.concatenate([kv_s[4 + 2 * g, krows, :], kv_s[4 + 2 * g + 1, krows, :]], axis=0)
            o = _dot(pmat, vmat)
            attn_parts.append(o * jnp.where(lane_lo, 1.0 / l_e, 1.0 / l_o))

            vst = jnp.concatenate([vnp_s[0, rows, cols], vnp_s[1, rows, cols]], axis=0)
            mixed = _dot(wsp_s[p], vst) + bs_ref[:, cols]
            gm_parts.append(u_s[rows, cols] * mixed)

        attn = jnp.concatenate(attn_parts, axis=1)
        gm = jnp.concatenate(gm_parts, axis=1)
        mix_s[rows, 0:ATTN_W] = _rms(attn, gatt_ref[...]).astype(BF16)
        mix_s[rows, ATTN_W:] = _rms(gm, ggm_ref[...]).astype(BF16)

    mixed = _dot(mix_s[...], wout_ref[...])
    h_ref[0] = x + _rms(mixed, gpost_ref[...])


def _const_spec(shape):
    nd = len(shape)
    return pl.BlockSpec(shape, lambda *_: (0,) * nd, pipeline_mode=pl.Buffered(1))


def _smem_spec():
    return pl.BlockSpec(memory_space=pltpu.SMEM)


def _prompt_mixer(x, w_in, b_in, rel_bias, sinks, ln_g, ln_b, w_s, bs_full, g_att, g_gm, w_out,
                  g_pre, g_post):
    B, L, _ = x.shape
    tile = PROMPT_TILE
    n_tiles = L // tile
    bucket = jnp.asarray(_prompt_bucket_tile())
    kern = functools.partial(_mixer_kernel, tile=tile, n_tiles=n_tiles)
    return pl.pallas_call(
        kern,
        grid=(B, n_tiles),
        in_specs=[
            pl.BlockSpec((1, tile, D_MODEL), lambda b, t: (b, t, 0)),
            _const_spec((D_MODEL, IN_W)),
            _const_spec((1, IN_W)),
            _smem_spec(),
            _smem_spec(),
            _const_spec((BLOCK, 2 * BLOCK)),
            _const_spec((1, GMLP_W)),
            _const_spec((1, GMLP_W)),
            _const_spec((N_HEADS, BLOCK, BLOCK)),
            _const_spec((BLOCK, GMLP_W)),
            _const_spec((1, ATTN_W)),
            _const_spec((1, GMLP_W)),
            _const_spec((D_MODEL, D_MODEL)),
            _const_spec((1, D_MODEL)),
            _const_spec((1, D_MODEL)),
        ],
        out_specs=[
            pl.BlockSpec((1, tile, D_MODEL), lambda b, t: (b, t, 0)),
            pl.BlockSpec((1, BLOCK, KV_W), lambda b, t: (b, 0, 0)),
            pl.BlockSpec((1, BLOCK, KV_W), lambda b, t: (b, 0, 0)),
        ],
        out_shape=[
            jax.ShapeDtypeStruct((B, L, D_MODEL), F32),
            jax.ShapeDtypeStruct((B, BLOCK, KV_W), F32),
            jax.ShapeDtypeStruct((B, BLOCK, KV_W), F32),
        ],
        scratch_shapes=[
            pltpu.VMEM((N_HEADS, BLOCK, 2 * BLOCK), F32),
            pltpu.VMEM((tile, ATTN_W), BF16),
            pltpu.VMEM((8, tile + BLOCK, LANES), BF16),
            pltpu.VMEM((tile, GMLP_W), F32),
            pltpu.VMEM((2, tile, GMLP_W), BF16),
            pltpu.VMEM((N_PAIRS, BLOCK, 2 * BLOCK), BF16),
            pltpu.VMEM((tile, D_MODEL), BF16),
        ],
        compiler_params=pltpu.CompilerParams(
            dimension_semantics=("arbitrary", "arbitrary"),
            vmem_limit_bytes=VMEM_LIMIT_BYTES),
        name="prompt_mixer",
    )(x, w_in, b_in, rel_bias, sinks, bucket, ln_g, ln_b, w_s, bs_full, g_att, g_gm, w_out,
      g_pre, g_post)


def _shift_rows(u, prev8, n):
    s = pltpu.roll(u, n, axis=0)
    top = jnp.where(lax.broadcasted_iota(jnp.int32, prev8.shape, 0) < n,
                    pltpu.roll(prev8, n, axis=0), s[0:SUBLANES])
    return jnp.concatenate([top, s[SUBLANES:]], axis=0)


def _ff_cols(gv, c):
    start = gv * D_FF + c * FF_CHUNK
    return slice(start, start + FF_CHUNK)


def _ffn_kernel(h_ref, wup_ref, cw_ref, cb_ref, wdown_ref, gpre_ref, gpost_ref,
                y_ref, cst_ref, h2_s, act_s, carry_s, *, tile):
    t = pl.program_id(1)

    @pl.when(t == 0)
    def _zero_state():
        carry_s[...] = jnp.zeros(carry_s.shape, F32)

    h = h_ref[0]
    h2_s[...] = _rms(h, gpre_ref[...]).astype(BF16)

    def up(c):
        return [_dot(h2_s[...], wup_ref[:, _ff_cols(gv, c)]) for gv in range(2)]

    def gate(c, us):
        conv = []
        for gv, u in enumerate(us):
            cols = _ff_cols(gv, c)
            prev8 = carry_s[:, cols]
            conv.append(cb_ref[:, cols] + cw_ref[0:1, cols] * _shift_rows(u, prev8, 2)
                        + cw_ref[1:2, cols] * _shift_rows(u, prev8, 1) + cw_ref[2:3, cols] * u)
            carry_s[:, cols] = u[tile - SUBLANES:, :]
        act_s[:, c * FF_CHUNK:(c + 1) * FF_CHUNK] = (_gelu_tanh(conv[0]) * conv[1]).astype(BF16)

    us_next = up(0)
    for c in range(N_FF_CHUNKS):
        us = us_next
        if c + 1 < N_FF_CHUNKS:
            us_next = up(c + 1)
        gate(c, us)

    cst_ref[0] = carry_s[...]
    y_ref[0] = h + _rms(_dot(act_s[...], wdown_ref[...]), gpost_ref[...])


def _prompt_ffn(h, w_up, conv_w, conv_b, w_down, g_pre, g_post):
    B, L, _ = h.shape
    tile = PROMPT_TILE
    n_tiles = L // tile
    kern = functools.partial(_ffn_kernel, tile=tile)
    return pl.pallas_call(
        kern,
        grid=(B, n_tiles),
        in_specs=[
            pl.BlockSpec((1, tile, D_MODEL), lambda b, t: (b, t, 0)),
            _const_spec((D_MODEL, 2 * D_FF)),
            _const_spec((3, 2 * D_FF)),
            _const_spec((1, 2 * D_FF)),
            _const_spec((D_FF, D_MODEL)),
            _const_spec((1, D_MODEL)),
            _const_spec((1, D_MODEL)),
        ],
        out_specs=[
            pl.BlockSpec((1, tile, D_MODEL), lambda b, t: (b, t, 0)),
            pl.BlockSpec((1, SUBLANES, 2 * D_FF), lambda b, t: (b, 0, 0)),
        ],
        out_shape=[
            jax.ShapeDtypeStruct((B, L, D_MODEL), F32),
            jax.ShapeDtypeStruct((B, SUBLANES, 2 * D_FF), F32),
        ],
        scratch_shapes=[
            pltpu.VMEM((tile, D_MODEL), BF16),
            pltpu.VMEM((tile, D_FF), BF16),
            pltpu.VMEM((SUBLANES, 2 * D_FF), F32),
        ],
        compiler_params=pltpu.CompilerParams(
            dimension_semantics=("arbitrary", "arbitrary"),
            vmem_limit_bytes=VMEM_LIMIT_BYTES),
        name="prompt_ffn",
    )(h, w_up, conv_w, conv_b, w_down, g_pre, g_post)


def _sample_attn_kernel(x_ref, wqkv_ref, bqkv_ref, gpre_ref, ck_ref, cv_ref, relbt_ref, sink_ref,
                        bucket_ref, attn_ref, wk_ref, wv_ref,
                        bias_s, kext_s, vext_s, rep_s, *, group):
    i = pl.program_id(0)
    rows = group * SUBLANES

    @pl.when(i == 0)
    def _init():
        bt = jnp.broadcast_to(bucket_ref[...], (N_HEADS, 2 * BLOCK))
        acc = jnp.zeros((N_HEADS, 2 * BLOCK), F32)
        for bk in range(NUM_BUCKETS):
            acc = jnp.where(bt == bk, relbt_ref[:, bk:bk + 1], acc)
        bias_s[...] = acc
        kext_s[...] = jnp.zeros(kext_s.shape, F32)
        vext_s[...] = jnp.zeros(vext_s.shape, F32)

    xn = _rms(x_ref[...], gpre_ref[...])
    xr = jnp.concatenate(
        [jnp.broadcast_to(xn[s:s + 1, :], (SUBLANES, D_MODEL)) for s in range(group)], axis=0)
    proj = _dot(xr.astype(BF16), wqkv_ref[...]) + bqkv_ref[...]
    q = proj[:, :ATTN_W] * (HEAD_DIM ** -0.5)
    k3 = proj[:, ATTN_W:ATTN_W + KV_W].reshape(group, SUBLANES, LANES)
    v3 = proj[:, ATTN_W + KV_W:].reshape(group, SUBLANES, LANES)

    head = lax.broadcasted_iota(jnp.int32, (rows, LANES), 0) & (SUBLANES - 1)
    lane_half = lax.broadcasted_iota(jnp.int32, (rows, LANES), 1) // HEAD_DIM
    lane_lo = lane_half == 0
    qpad = jnp.zeros((rows, LANES), F32)
    for p in range(N_PAIRS):
        g = p // 2
        qp = q[:, p * LANES:(p + 1) * LANES]
        qr = pltpu.roll(qp, HEAD_DIM, axis=1)
        src_even = qp if g == 0 else qr
        src_odd = qr if g == 0 else qp
        qpad = jnp.where((head == 2 * p) & (lane_half == g), src_even, qpad)
        qpad = jnp.where((head == 2 * p + 1) & (lane_half == g), src_odd, qpad)
    qpad = qpad.reshape(group, SUBLANES, LANES).astype(BF16)

    ck = ck_ref[...]
    cv = cv_ref[...]
    kext_s[:, 0:BLOCK, :] = ck
    vext_s[:, 0:BLOCK, :] = cv
    kext_s[:, BLOCK:BLOCK + SUBLANES, :] = k3
    vext_s[:, BLOCK:BLOCK + SUBLANES, :] = v3

    s = jnp.einsum('shc,sjc->shj', qpad, kext_s[...].astype(BF16), preferred_element_type=F32)
    allowed = jnp.broadcast_to(bucket_ref[...], (N_HEADS, 2 * BLOCK)) >= 0
    s = jnp.where(allowed[None], s + bias_s[...][None], NEG_INF)
    pr, l = _softmax_with_sink(s, sink_ref[...][None])
    o = jnp.einsum('shj,sjc->shc', pr.astype(BF16), vext_s[...].astype(BF16),
                   preferred_element_type=F32)
    o = (o * (1.0 / l)).reshape(rows, LANES)

    om = jnp.where(lane_half == head // (N_HEADS // N_KV), o, 0.0)
    both = (om + pltpu.roll(om, HEAD_DIM, axis=1)).reshape(group, SUBLANES, LANES)
    lane_lo3 = lane_lo.reshape(group, SUBLANES, LANES)
    for p in range(N_PAIRS):
        even = jnp.broadcast_to(both[:, 2 * p:2 * p + 1, :], (group, SUBLANES, LANES))
        odd = jnp.broadcast_to(both[:, 2 * p + 1:2 * p + 2, :], (group, SUBLANES, LANES))
        rep_s[p] = jnp.where(lane_lo3, even, odd).reshape(rows, LANES)
    for p in range(N_PAIRS):
        attn_ref[:, p * LANES:(p + 1) * LANES] = rep_s[p, pl.ds(0, group, stride=SUBLANES), :]

    last = lax.broadcasted_iota(jnp.int32, (group, SUBLANES, LANES), 1) == SUBLANES - 1
    for cache, new3, out_ref in ((ck, k3, wk_ref), (cv, v3, wv_ref)):
        shifted = pltpu.roll(cache, BLOCK - 1, axis=1)
        out_ref[:, 0:BLOCK - SUBLANES, :] = shifted[:, 0:BLOCK - SUBLANES, :]
        out_ref[:, BLOCK - SUBLANES:, :] = jnp.where(last, new3, shifted[:, BLOCK - SUBLANES:, :])


def _sample_attention(x, wqkv, bqkv, g_pre, cache_k, cache_v, relb_t, sinks_col):
    S = x.shape[0]
    group = SAMPLE_GROUP
    bucket = jnp.asarray(_sample_bucket_row())
    kern = functools.partial(_sample_attn_kernel, group=group)
    return pl.pallas_call(
        kern,
        grid=(S // group,),
        in_specs=[
            pl.BlockSpec((group, D_MODEL), lambda i: (i, 0)),
            _const_spec((D_MODEL, QKV_W)),
            _const_spec((1, QKV_W)),
            _const_spec((1, D_MODEL)),
            pl.BlockSpec((group, BLOCK, KV_W), lambda i: (i, 0, 0)),
            pl.BlockSpec((group, BLOCK, KV_W), lambda i: (i, 0, 0)),
            _const_spec((N_HEADS, NUM_BUCKETS)),
            _const_spec((N_HEADS, 1)),
            _const_spec((1, 2 * BLOCK)),
        ],
        out_specs=[
            pl.BlockSpec((group, ATTN_W), lambda i: (i, 0)),
            pl.BlockSpec((group, BLOCK, KV_W), lambda i: (i, 0, 0)),
            pl.BlockSpec((group, BLOCK, KV_W), lambda i: (i, 0, 0)),
        ],
        out_shape=[
            jax.ShapeDtypeStruct((S, ATTN_W), F32),
            jax.ShapeDtypeStruct((S, BLOCK, KV_W), F32),
            jax.ShapeDtypeStruct((S, BLOCK, KV_W), F32),
        ],
        scratch_shapes=[
            pltpu.VMEM((N_HEADS, 2 * BLOCK), F32),
            pltpu.VMEM((group, 2 * BLOCK, LANES), F32),
            pltpu.VMEM((group, 2 * BLOCK, LANES), F32),
            pltpu.VMEM((N_PAIRS, group * SUBLANES, LANES), F32),
        ],
        compiler_params=pltpu.CompilerParams(
            dimension_semantics=("arbitrary",),
            vmem_limit_bytes=VMEM_LIMIT_BYTES),
        name="sample_attention",
    )(x, wqkv, bqkv, g_pre, cache_k, cache_v, relb_t, sinks_col, bucket)


def _sample_tail_kernel(x_ref, attn_ref, st_ref, win_ref, bin_ref, lng_ref, lnb_ref, ws0_ref, bs0_ref,
                        gatt_ref, ggm_ref, wout_ref, gpre_ref, gpost_ref, gpre2_ref, gpost2_ref,
                        wup_ref, cw_ref, cb_ref, wdown_ref,
                        y_ref, vn_ref, nst_ref):
    x = x_ref[...]
    xn = _rms(x, gpre_ref[...])
    puv = _dot(xn.astype(BF16), win_ref[:, QKV_W:]) + bin_ref[:, QKV_W:]
    u = _gelu_erf(puv[:, :GMLP_W])
    vn = _layer_norm(_gelu_erf(puv[:, GMLP_W:]), lng_ref[...], lnb_ref[...])
    vn_ref[...] = vn
    gm = u * (ws0_ref[...] * vn + bs0_ref[...])
    mix = jnp.concatenate([_rms(attn_ref[...], gatt_ref[...]), _rms(gm, ggm_ref[...])], axis=1)
    h = x + _rms(_dot(mix.astype(BF16), wout_ref[...]), gpost_ref[...])

    h2 = _rms(h, gpre2_ref[...]).astype(BF16)
    two_ff = 2 * D_FF
    nst_ref[:, 0:two_ff] = st_ref[:, two_ff:]
    acc = jnp.zeros((x.shape[0], D_MODEL), F32)
    for c in range(N_FF_CHUNKS):
        conv = []
        for gv in range(2):
            cols = _ff_cols(gv, c)
            cols1 = slice(two_ff + cols.start, two_ff + cols.stop)
            u_c = _dot(h2, wup_ref[:, cols])
            conv.append(cb_ref[:, cols] + cw_ref[0:1, cols] * st_ref[:, cols]
                        + cw_ref[1:2, cols] * st_ref[:, cols1] + cw_ref[2:3, cols] * u_c)
            nst_ref[:, cols1] = u_c
        act = (_gelu_tanh(conv[0]) * conv[1]).astype(BF16)
        acc = acc + _dot(act, wdown_ref[c * FF_CHUNK:(c + 1) * FF_CHUNK, :])
    y_ref[...] = h + _rms(acc, gpost2_ref[...])


def _sample_tail(x, attn, state, w_in, b_in, ln_g, ln_b, ws0, bs0, g_att, g_gm, w_out, g_pre, g_post,
                 g_pre2, g_post2, w_up, conv_w, conv_b, w_down):
    S = x.shape[0]
    args = (x, attn, state, w_in, b_in, ln_g, ln_b, ws0, bs0, g_att, g_gm, w_out, g_pre, g_post,
            g_pre2, g_post2, w_up, conv_w, conv_b, w_down)
    return pl.pallas_call(
        _sample_tail_kernel,
        grid=(1,),
        in_specs=[_const_spec(a.shape) for a in args],
        out_specs=[
            pl.BlockSpec((S, D_MODEL), lambda i: (0, 0)),
            pl.BlockSpec((S, GMLP_W), lambda i: (0, 0)),
            pl.BlockSpec((S, 4 * D_FF), lambda i: (0, 0)),
        ],
        out_shape=[
            jax.ShapeDtypeStruct((S, D_MODEL), F32),
            jax.ShapeDtypeStruct((S, GMLP_W), F32),
            jax.ShapeDtypeStruct((S, 4 * D_FF), F32),
        ],
        compiler_params=pltpu.CompilerParams(
            dimension_semantics=("arbitrary",),
            vmem_limit_bytes=VMEM_LIMIT_BYTES),
        name="sample_tail",
    )(*args)


def kernel(x_prompt, x_sample, cache_win_k, cache_win_v, state_ffn_conv, rel_bias, w_in, b_in, attn_sinks, gmlp_ln_g, gmlp_ln_b, gmlp_w_s, gmlp_b_s, g_attn_out, g_gmlp_out, w_out, g_pre_mix, g_post_mix, g_pre_ffn, g_post_ffn, w_up, ffn_conv_w, ffn_conv_b, w_down):
    depth = w_in.shape[0]
    assert depth == 1, "single-layer step"
    B, L, _ = x_prompt.shape
    S = x_sample.shape[0]
    assert x_sample.shape[1] == 1 and L % PROMPT_TILE == 0 and S % SAMPLE_GROUP == 0
    assert cache_win_k.shape[2] == BLOCK

    row = lambda a: a.reshape(1, -1)
    w_in_b = w_in[0].astype(BF16)
    w_out_b = w_out[0].astype(BF16)
    b_in_r = row(b_in[0])
    w_up_b = w_up[0].astype(BF16)
    w_down_b = w_down[0].astype(BF16)
    conv_w, conv_b = ffn_conv_w[0], row(ffn_conv_b[0])
    bs_full = jnp.repeat(gmlp_b_s[0].T, HEAD_DIM, axis=1)
    ln_g, ln_b = row(gmlp_ln_g[0]), row(gmlp_ln_b[0])
    g_att, g_gm = row(g_attn_out[0]), row(g_gmlp_out[0])
    g_pre, g_post = row(g_pre_mix[0]), row(g_post_mix[0])
    g_pre2, g_post2 = row(g_pre_ffn[0]), row(g_post_ffn[0])

    h_p, wk_p, wv_p = _prompt_mixer(x_prompt, w_in_b, b_in_r, rel_bias, attn_sinks[0], ln_g, ln_b,
                                    gmlp_w_s[0], bs_full, g_att, g_gm, w_out_b, g_pre, g_post)
    y_p, cst_p = _prompt_ffn(h_p, w_up_b, conv_w, conv_b, w_down_b, g_pre2, g_post2)
    conv_p = cst_p[:, SUBLANES - 2:, :].reshape(1, B, 2, 2 * D_FF)

    xs = x_sample.reshape(S, D_MODEL)
    attn_s, wk_s, wv_s = _sample_attention(
        xs, w_in_b, b_in_r, g_pre,
        cache_win_k[0].reshape(S, BLOCK, KV_W), cache_win_v[0].reshape(S, BLOCK, KV_W),
        rel_bias.T, attn_sinks[0].reshape(N_HEADS, 1))
    ws0 = row(jnp.repeat(gmlp_w_s[0][:, 0, 0], HEAD_DIM))
    bs0 = row(jnp.repeat(gmlp_b_s[0][:, 0], HEAD_DIM))
    y_s, vn_s, nst_s = _sample_tail(
        xs, attn_s, state_ffn_conv[0].reshape(S, 4 * D_FF), w_in_b, b_in_r,
        ln_g, ln_b, ws0, bs0, g_att, g_gm, w_out_b, g_pre, g_post, g_pre2, g_post2,
        w_up_b, conv_w, conv_b, w_down_b)

    kv_shape_p = (1, B, BLOCK, N_KV, HEAD_DIM)
    kv_shape_s = (1, S, BLOCK, N_KV, HEAD_DIM)
    return (y_p,
            y_s.reshape(S, 1, D_MODEL),
            wk_p.reshape(kv_shape_p), wv_p.reshape(kv_shape_p),
            wk_s.reshape(kv_shape_s), wv_s.reshape(kv_shape_s),
            vn_s.reshape(1, S, 1, GMLP_W),
            conv_p,
            nst_s.reshape(1, S, 2, 2 * D_FF))
```

```python
import functools
import math

import numpy as np
import jax
import jax.numpy as jnp
from jax import lax
from jax.experimental import pallas as pl
from jax.experimental.pallas import tpu as pltpu

D_MODEL = 1024
HEAD_DIM = 64
N_HEADS = 8
N_KV = 2
N_PAIRS = N_HEADS // 2
BLOCK = 128
ATTN_W = N_HEADS * HEAD_DIM
KV_W = N_KV * HEAD_DIM
GMLP_W = 512
IN_W = ATTN_W + 2 * KV_W + 2 * GMLP_W
QKV_W = ATTN_W + 2 * KV_W
D_FF = 2816
FF_CHUNK = 256
N_FF_CHUNKS = D_FF // FF_CHUNK
NUM_BUCKETS = 32
MAX_DISTANCE = 128
EPS = 1e-6
NEG_INF = -1e30
SQRT_HALF = np.sqrt(0.5).astype(np.float32)
SQRT_2_OVER_PI = np.sqrt(2 / np.pi).astype(np.float32)

LANES = 128
SUBLANES = 8
VMEM_LIMIT_BYTES = 58 * 1024 * 1024

PROMPT_TILE = 512
SAMPLE_GROUP = 16

F32 = jnp.float32
BF16 = jnp.bfloat16


def _t5_bucket_np(dist):
    n = np.maximum(dist, 0)
    max_exact = NUM_BUCKETS // 2
    nf = np.maximum(n, 1).astype(np.float32)
    large = max_exact + (np.log(nf / max_exact) / math.log(MAX_DISTANCE / max_exact)
                         * (NUM_BUCKETS - max_exact)).astype(np.int32)
    large = np.minimum(large, NUM_BUCKETS - 1)
    return np.where(n < max_exact, n, large).astype(np.int32)


def _prompt_bucket_tile():
    r = np.arange(BLOCK)[:, None]
    c = np.arange(2 * BLOCK)[None, :]
    dist = r + BLOCK - c
    ok = (dist >= 0) & (dist <= BLOCK)
    return np.where(ok, _t5_bucket_np(dist), -1).astype(np.int32)


def _sample_bucket_row():
    c = np.arange(2 * BLOCK)[None, :]
    dist = BLOCK - c
    ok = dist >= 0
    return np.where(ok, _t5_bucket_np(dist), -1).astype(np.int32)


def _rms(x, g):
    ms = jnp.mean(x * x, axis=-1, keepdims=True)
    return x * lax.rsqrt(ms + EPS) * g


def _layer_norm(x, g, b):
    mu = jnp.mean(x, axis=-1, keepdims=True)
    xc = x - mu
    y = xc * lax.rsqrt(jnp.mean(xc * xc, axis=-1, keepdims=True) + EPS)
    return y * g + b


def _gelu_erf(x):
    return 0.5 * x * (1.0 + lax.erf(x * SQRT_HALF))


def _gelu_tanh(x):
    cdf = 0.5 * (1.0 + jnp.tanh(SQRT_2_OVER_PI * (x + 0.044715 * (x * x * x))))
    return x * cdf


def _softmax_with_sink(s, sink):
    m = jnp.maximum(jnp.max(s, axis=-1, keepdims=True), sink)
    p = jnp.exp(s - m)
    l = jnp.sum(p, axis=-1, keepdims=True) + jnp.exp(sink - m)
    return p, l


def _dot(a, b):
    return jnp.dot(a, b, preferred_element_type=F32)


def _dot_nt(a, b):
    return lax.dot_general(a, b, (((1,), (1,)), ((), ())), preferred_element_type=F32)


def _shift_rows(u, prev8, n):
    s = pltpu.roll(u, n, axis=0)
    top = jnp.where(lax.broadcasted_iota(jnp.int32, prev8.shape, 0) < n,
                    pltpu.roll(prev8, n, axis=0), s[0:SUBLANES])
    return jnp.concatenate([top, s[SUBLANES:]], axis=0)


def _ff_cols(gv, c):
    start = gv * D_FF + c * FF_CHUNK
    return slice(start, start + FF_CHUNK)


def _const_spec(shape):
    nd = len(shape)
    return pl.BlockSpec(shape, lambda *_: (0,) * nd, pipeline_mode=pl.Buffered(1))


def _smem_spec():
    return pl.BlockSpec(memory_space=pltpu.SMEM)


_INTERLEAVE = "mfmffmffmffmffmffmff"


def _prompt_kernel(x_ref, win_ref, bin_ref, relb_ref, sink_ref, bucket_ref, lng_ref, lnb_ref,
                   ws_ref, bs_ref, gatt_ref, ggm_ref, wout_ref, gpre_ref, gpost_ref,
                   wup_ref, cw_ref, cb_ref, wdown_ref, gpre2_ref, gpost2_ref,
                   y_ref, wk_ref, wv_ref, cst_ref,
                   bias_s, q_s, kv_s, u_s, vnp_s, wsp_s, mix_s, h_s, h2_s, act_s, carry_s,
                   *, tile, n_tiles, n_total):
    step = pl.program_id(0)
    n_blk = tile // BLOCK
    t_mix = jnp.minimum(step, n_total - 1) % n_tiles
    t_ffn = jnp.maximum(step - 1, 0) % n_tiles
    slot = step % 2

    @pl.when(step == 0)
    def _init():
        bt = bucket_ref[...]
        for h in range(N_HEADS):
            acc = jnp.zeros((BLOCK, 2 * BLOCK), F32)
            for bk in range(NUM_BUCKETS):
                acc = jnp.where(bt == bk, relb_ref[bk, h], acc)
            bias_s[h] = acc
        row = lax.broadcasted_iota(jnp.int32, (BLOCK, BLOCK), 0)
        col = lax.broadcasted_iota(jnp.int32, (BLOCK, BLOCK), 1)
        tril = (row >= col).astype(F32)
        for p in range(N_PAIRS):
            wsp_s[p] = jnp.concatenate(
                [ws_ref[2 * p] * tril, ws_ref[2 * p + 1] * tril], axis=1).astype(BF16)
        kv_s[...] = jnp.zeros(kv_s.shape, BF16)
        carry_s[...] = jnp.zeros(carry_s.shape, F32)
        h_s[...] = jnp.zeros(h_s.shape, F32)

    def mixer():
        prev = kv_s[:, tile:tile + BLOCK, :]
        kv_s[:, 0:BLOCK, :] = jnp.where(t_mix > 0, prev, jnp.zeros_like(prev))

        x = x_ref[0]
        xn = _rms(x, gpre_ref[...])
        proj = _dot(xn.astype(BF16), win_ref[...]) + bin_ref[...]
        q_s[...] = (proj[:, :ATTN_W] * (HEAD_DIM ** -0.5)).astype(BF16)
        k = proj[:, ATTN_W:ATTN_W + KV_W]
        v = proj[:, ATTN_W + KV_W:QKV_W]
        wk_ref[0] = k[tile - BLOCK:, :]
        wv_ref[0] = v[tile - BLOCK:, :]

        lo = lax.broadcasted_iota(jnp.int32, (tile, LANES), 1) < HEAD_DIM
        for base, val in ((0, k), (4, v)):
            rolled = pltpu.roll(val, HEAD_DIM, axis=1)
            kv_s[base + 0, BLOCK:, :] = jnp.where(lo, val, 0.0).astype(BF16)
            kv_s[base + 1, BLOCK:, :] = jnp.where(lo, 0.0, rolled).astype(BF16)
            kv_s[base + 2, BLOCK:, :] = jnp.where(lo, rolled, 0.0).astype(BF16)
            kv_s[base + 3, BLOCK:, :] = jnp.where(lo, 0.0, val).astype(BF16)
        yield

        u_s[...] = _gelu_erf(proj[:, QKV_W:QKV_W + GMLP_W])
        vn = _layer_norm(_gelu_erf(proj[:, QKV_W + GMLP_W:]), lng_ref[...], lnb_ref[...])
        lo4 = (lax.broadcasted_iota(jnp.int32, (tile, GMLP_W), 1) & (LANES - 1)) < HEAD_DIM
        vnp_s[0] = jnp.where(lo4, vn, 0.0).astype(BF16)
        vnp_s[1] = jnp.where(lo4, 0.0, vn).astype(BF16)
        yield

        r = lax.broadcasted_iota(jnp.int32, (BLOCK, 2 * BLOCK), 0)
        c = lax.broadcasted_iota(jnp.int32, (BLOCK, 2 * BLOCK), 1)
        dist = r + BLOCK - c
        in_band = (dist >= 0) & (dist <= BLOCK)
        lane_lo = lax.broadcasted_iota(jnp.int32, (BLOCK, LANES), 1) < HEAD_DIM

        for j in range(n_blk):
            rows = slice(j * BLOCK, (j + 1) * BLOCK)
            krows = slice(j * BLOCK, (j + 2) * BLOCK)
            if j == 0:
                allowed = in_band & (c >= jnp.where(t_mix > 0, 0, BLOCK))
            else:
                allowed = in_band

            attn_parts = []
            gm_parts = []
            for p in range(N_PAIRS):
                g = p // 2
                cols = slice(p * LANES, (p + 1) * LANES)
                qp = q_s[rows, cols]
                parts = []
                for par in range(2):
                    h = 2 * p + par
                    s = _dot_nt(qp, kv_s[2 * g + par, krows, :])
                    s = jnp.where(allowed, s + bias_s[h], NEG_INF)
                    parts.append(_softmax_with_sink(s, sink_ref[h]))
                (p_e, l_e), (p_o, l_o) = parts
                pmat = jnp.concatenate([p_e.astype(BF16), p_o.astype(BF16)], axis=1)
                vmat = jnp.concatenate(
                    [kv_s[4 + 2 * g, krows, :], kv_s[4 + 2 * g + 1, krows, :]], axis=0)
                o = _dot(pmat, vmat)
                attn_parts.append(o * jnp.where(lane_lo, 1.0 / l_e, 1.0 / l_o))

                vst = jnp.concatenate([vnp_s[0, rows, cols], vnp_s[1, rows, cols]], axis=0)
                mixed = _dot(wsp_s[p], vst) + bs_ref[:, cols]
                gm_parts.append(u_s[rows, cols] * mixed)

            attn = jnp.concatenate(attn_parts, axis=1)
            gm = jnp.concatenate(gm_parts, axis=1)
            mix_s[rows, 0:ATTN_W] = _rms(attn, gatt_ref[...]).astype(BF16)
            mix_s[rows, ATTN_W:] = _rms(gm, ggm_ref[...]).astype(BF16)
            yield

        mixed = _dot(mix_s[...], wout_ref[...])
        h_s[slot] = x + _rms(mixed, gpost_ref[...])
        yield

    def conv_ffn():
        h = h_s[1 - slot]
        h2_s[...] = _rms(h, gpre2_ref[...]).astype(BF16)

        def up(c):
            return [_dot(h2_s[...], wup_ref[:, _ff_cols(gv, c)]) for gv in range(2)]

        def gate(c, us):
            conv = []
            for gv, u in enumerate(us):
                cols = _ff_cols(gv, c)
                prev8 = jnp.where(t_ffn > 0, carry_s[:, cols], 0.0)
                conv.append(cb_ref[:, cols] + cw_ref[0:1, cols] * _shift_rows(u, prev8, 2)
                            + cw_ref[1:2, cols] * _shift_rows(u, prev8, 1) + cw_ref[2:3, cols] * u)
                carry_s[:, cols] = u[tile - SUBLANES:, :]
            act_s[:, c * FF_CHUNK:(c + 1) * FF_CHUNK] = (
                _gelu_tanh(conv[0]) * conv[1]).astype(BF16)

        us_next = up(0)
        yield
        for c in range(N_FF_CHUNKS):
            us = us_next
            if c + 1 < N_FF_CHUNKS:
                us_next = up(c + 1)
            gate(c, us)
            yield

        cst_ref[0] = carry_s[...]
        y_ref[0] = h + _rms(_dot(act_s[...], wdown_ref[...]), gpost2_ref[...])
        yield

    streams = {"m": mixer(), "f": conv_ffn()}
    for which in _INTERLEAVE:
        next(streams[which])


def _prompt_layer(x, w_in, b_in, rel_bias, sinks, ln_g, ln_b, w_s, bs_full, g_att, g_gm, w_out,
                  g_pre, g_post, w_up, conv_w, conv_b, w_down, g_pre2, g_post2):
    B, L, _ = x.shape
    tile = PROMPT_TILE
    n_tiles = L // tile
    n_total = B * n_tiles
    n_blk = tile // BLOCK
    assert _INTERLEAVE.count("m") == n_blk + 3 and _INTERLEAVE.count("f") == N_FF_CHUNKS + 2
    bucket = jnp.asarray(_prompt_bucket_tile())

    def mix_tile(s):
        return jnp.minimum(s, n_total - 1)

    def ffn_tile(s):
        return jnp.maximum(s - 1, 0)

    kern = functools.partial(_prompt_kernel, tile=tile, n_tiles=n_tiles, n_total=n_total)
    return pl.pallas_call(
        kern,
        grid=(n_total + 1,),
        in_specs=[
            pl.BlockSpec((1, tile, D_MODEL), lambda s: (mix_tile(s) // n_tiles, mix_tile(s) % n_tiles, 0)),
            _const_spec((D_MODEL, IN_W)),
            _const_spec((1, IN_W)),
            _smem_spec(),
            _smem_spec(),
            _const_spec((BLOCK, 2 * BLOCK)),
            _const_spec((1, GMLP_W)),
            _const_spec((1, GMLP_W)),
            _const_spec((N_HEADS, BLOCK, BLOCK)),
            _const_spec((BLOCK, GMLP_W)),
            _const_spec((1, ATTN_W)),
            _const_spec((1, GMLP_W)),
            _const_spec((D_MODEL, D_MODEL)),
            _const_spec((1, D_MODEL)),
            _const_spec((1, D_MODEL)),
            _const_spec((D_MODEL, 2 * D_FF)),
            _const_spec((3, 2 * D_FF)),
            _const_spec((1, 2 * D_FF)),
            _const_spec((D_FF, D_MODEL)),
            _const_spec((1, D_MODEL)),
            _const_spec((1, D_MODEL)),
        ],
        out_specs=[
            pl.BlockSpec((1, tile, D_MODEL), lambda s: (ffn_tile(s) // n_tiles, ffn_tile(s) % n_tiles, 0)),
            pl.BlockSpec((1, BLOCK, KV_W), lambda s: (mix_tile(s) // n_tiles, 0, 0)),
            pl.BlockSpec((1, BLOCK, KV_W), lambda s: (mix_tile(s) // n_tiles, 0, 0)),
            pl.BlockSpec((1, SUBLANES, 2 * D_FF), lambda s: (ffn_tile(s) // n_tiles, 0, 0)),
        ],
        out_shape=[
            jax.ShapeDtypeStruct((B, L, D_MODEL), F32),
            jax.ShapeDtypeStruct((B, BLOCK, KV_W), F32),
            jax.ShapeDtypeStruct((B, BLOCK, KV_W), F32),
            jax.ShapeDtypeStruct((B, SUBLANES, 2 * D_FF), F32),
        ],
        scratch_shapes=[
            pltpu.VMEM((N_HEADS, BLOCK, 2 * BLOCK), F32),
            pltpu.VMEM((tile, ATTN_W), BF16),
            pltpu.VMEM((8, tile + BLOCK, LANES), BF16),
            pltpu.VMEM((tile, GMLP_W), F32),
            pltpu.VMEM((2, tile, GMLP_W), BF16),
            pltpu.VMEM((N_PAIRS, BLOCK, 2 * BLOCK), BF16),
            pltpu.VMEM((tile, D_MODEL), BF16),
            pltpu.VMEM((2, tile, D_MODEL), F32),
            pltpu.VMEM((tile, D_MODEL), BF16),
            pltpu.VMEM((tile, D_FF), BF16),
            pltpu.VMEM((SUBLANES, 2 * D_FF), F32),
        ],
        compiler_params=pltpu.CompilerParams(
            dimension_semantics=("arbitrary",),
            vmem_limit_bytes=VMEM_LIMIT_BYTES),
        name="prompt_layer",
    )(x, w_in, b_in, rel_bias, sinks, bucket, ln_g, ln_b, w_s, bs_full, g_att, g_gm, w_out,
      g_pre, g_post, w_up, conv_w, conv_b, w_down, g_pre2, g_post2)


def _sample_attn_kernel(x_ref, wqkv_ref, bqkv_ref, gpre_ref, ck_ref, cv_ref, relbt_ref, sink_ref,
                        bucket_ref, attn_ref, wk_ref, wv_ref,
                        bias_s, kext_s, vext_s, rep_s, *, group):
    i = pl.program_id(0)
    rows = group * SUBLANES

    @pl.when(i == 0)
    def _init():
        bt = jnp.broadcast_to(bucket_ref[...], (N_HEADS, 2 * BLOCK))
        acc = jnp.zeros((N_HEADS, 2 * BLOCK), F32)
        for bk in range(NUM_BUCKETS):
            acc = jnp.where(bt == bk, relbt_ref[:, bk:bk + 1], acc)
        bias_s[...] = acc
        kext_s[...] = jnp.zeros(kext_s.shape, F32)
        vext_s[...] = jnp.zeros(vext_s.shape, F32)

    xn = _rms(x_ref[...], gpre_ref[...])
    xr = jnp.concatenate(
        [jnp.broadcast_to(xn[s:s + 1, :], (SUBLANES, D_MODEL)) for s in range(group)], axis=0)
    proj = _dot(xr.astype(BF16), wqkv_ref[...]) + bqkv_ref[...]
    q = proj[:, :ATTN_W] * (HEAD_DIM ** -0.5)
    k3 = proj[:, ATTN_W:ATTN_W + KV_W].reshape(group, SUBLANES, LANES)
    v3 = proj[:, ATTN_W + KV_W:].reshape(group, SUBLANES, LANES)

    head = lax.broadcasted_iota(jnp.int32, (rows, LANES), 0) & (SUBLANES - 1)
    lane_half = lax.broadcasted_iota(jnp.int32, (rows, LANES), 1) // HEAD_DIM
    lane_lo = lane_half == 0
    qpad = jnp.zeros((rows, LANES), F32)
    for p in range(N_PAIRS):
        g = p // 2
        qp = q[:, p * LANES:(p + 1) * LANES]
        qr = pltpu.roll(qp, HEAD_DIM, axis=1)
        src_even = qp if g == 0 else qr
        src_odd = qr if g == 0 else qp
        qpad = jnp.where((head == 2 * p) & (lane_half == g), src_even, qpad)
        qpad = jnp.where((head == 2 * p + 1) & (lane_half == g), src_odd, qpad)
    qpad = qpad.reshape(group, SUBLANES, LANES).astype(BF16)

    ck = ck_ref[...]
    cv = cv_ref[...]
    kext_s[:, 0:BLOCK, :] = ck
    vext_s[:, 0:BLOCK, :] = cv
    kext_s[:, BLOCK:BLOCK + SUBLANES, :] = k3
    vext_s[:, BLOCK:BLOCK + SUBLANES, :] = v3

    s = jnp.einsum('shc,sjc->shj', qpad, kext_s[...].astype(BF16), preferred_element_type=F32)
    allowed = jnp.broadcast_to(bucket_ref[...], (N_HEADS, 2 * BLOCK)) >= 0
    s = jnp.where(allowed[None], s + bias_s[...][None], NEG_INF)
    pr, l = _softmax_with_sink(s, sink_ref[...][None])
    o = jnp.einsum('shj,sjc->shc', pr.astype(BF16), vext_s[...].astype(BF16),
                   preferred_element_type=F32)
    o = (o * (1.0 / l)).reshape(rows, LANES)

    om = jnp.where(lane_half == head // (N_HEADS // N_KV), o, 0.0)
    both = (om + pltpu.roll(om, HEAD_DIM, axis=1)).reshape(group, SUBLANES, LANES)
    lane_lo3 = lane_lo.reshape(group, SUBLANES, LANES)
    for p in range(N_PAIRS):
        even = jnp.broadcast_to(both[:, 2 * p:2 * p + 1, :], (group, SUBLANES, LANES))
        odd = jnp.broadcast_to(both[:, 2 * p + 1:2 * p + 2, :], (group, SUBLANES, LANES))
        rep_s[p] = jnp.where(lane_lo3, even, odd).reshape(rows, LANES)
    for p in range(N_PAIRS):
        attn_ref[:, p * LANES:(p + 1) * LANES] = rep_s[p, pl.ds(0, group, stride=SUBLANES), :]

    last = lax.broadcasted_iota(jnp.int32, (group, SUBLANES, LANES), 1) == SUBLANES - 1
    for cache, new3, out_ref in ((ck, k3, wk_ref), (cv, v3, wv_ref)):
        shifted = pltpu.roll(cache, BLOCK - 1, axis=1)
        out_ref[:, 0:BLOCK - SUBLANES, :] = shifted[:, 0:BLOCK - SUBLANES, :]
        out_ref[:, BLOCK - SUBLANES:, :] = jnp.where(last, new3, shifted[:, BLOCK - SUBLANES:, :])


def _sample_attention(x, wqkv, bqkv, g_pre, cache_k, cache_v, relb_t, sinks_col):
    S = x.shape[0]
    group = SAMPLE_GROUP
    bucket = jnp.asarray(_sample_bucket_row())
    kern = functools.partial(_sample_attn_kernel, group=group)
    return pl.pallas_call(
        kern,
        grid=(S // group,),
        in_specs=[
            pl.BlockSpec((group, D_MODEL), lambda i: (i, 0)),
            _const_spec((D_MODEL, QKV_W)),
            _const_spec((1, QKV_W)),
            _const_spec((1, D_MODEL)),
            pl.BlockSpec((group, BLOCK, KV_W), lambda i: (i, 0, 0)),
            pl.BlockSpec((group, BLOCK, KV_W), lambda i: (i, 0, 0)),
            _const_spec((N_HEADS, NUM_BUCKETS)),
            _const_spec((N_HEADS, 1)),
            _const_spec((1, 2 * BLOCK)),
        ],
        out_specs=[
            pl.BlockSpec((group, ATTN_W), lambda i: (i, 0)),
            pl.BlockSpec((group, BLOCK, KV_W), lambda i: (i, 0, 0)),
            pl.BlockSpec((group, BLOCK, KV_W), lambda i: (i, 0, 0)),
        ],
        out_shape=[
            jax.ShapeDtypeStruct((S, ATTN_W), F32),
            jax.ShapeDtypeStruct((S, BLOCK, KV_W), F32),
            jax.ShapeDtypeStruct((S, BLOCK, KV_W), F32),
        ],
        scratch_shapes=[
            pltpu.VMEM((N_HEADS, 2 * BLOCK), F32),
            pltpu.VMEM((group, 2 * BLOCK, LANES), F32),
            pltpu.VMEM((group, 2 * BLOCK, LANES), F32),
            pltpu.VMEM((N_PAIRS, group * SUBLANES, LANES), F32),
        ],
        compiler_params=pltpu.CompilerParams(
            dimension_semantics=("arbitrary",),
            vmem_limit_bytes=VMEM_LIMIT_BYTES),
        name="sample_attention",
    )(x, wqkv, bqkv, g_pre, cache_k, cache_v, relb_t, sinks_col, bucket)


def _sample_tail_kernel(x_ref, attn_ref, st_ref, win_ref, bin_ref, lng_ref, lnb_ref, ws0_ref, bs0_ref,
                        gatt_ref, ggm_ref, wout_ref, gpre_ref, gpost_ref, gpre2_ref, gpost2_ref,
                        wup_ref, cw_ref, cb_ref, wdown_ref,
                        y_ref, vn_ref, nst_ref):
    x = x_ref[...]
    xn = _rms(x, gpre_ref[...])
    puv = _dot(xn.astype(BF16), win_ref[:, QKV_W:]) + bin_ref[:, QKV_W:]
    u = _gelu_erf(puv[:, :GMLP_W])
    vn = _layer_norm(_gelu_erf(puv[:, GMLP_W:]), lng_ref[...], lnb_ref[...])
    vn_ref[...] = vn
    gm = u * (ws0_ref[...] * vn + bs0_ref[...])
    mix = jnp.concatenate([_rms(attn_ref[...], gatt_ref[...]), _rms(gm, ggm_ref[...])], axis=1)
    h = x + _rms(_dot(mix.astype(BF16), wout_ref[...]), gpost_ref[...])

    h2 = _rms(h, gpre2_ref[...]).astype(BF16)
    two_ff = 2 * D_FF
    nst_ref[:, 0:two_ff] = st_ref[:, two_ff:]
    acc = jnp.zeros((x.shape[0], D_MODEL), F32)
    for c in range(N_FF_CHUNKS):
        conv = []
        for gv in range(2):
            cols = _ff_cols(gv, c)
            cols1 = slice(two_ff + cols.start, two_ff + cols.stop)
            u_c = _dot(h2, wup_ref[:, cols])
            conv.append(cb_ref[:, cols] + cw_ref[0:1, cols] * st_ref[:, cols]
                        + cw_ref[1:2, cols] * st_ref[:, cols1] + cw_ref[2:3, cols] * u_c)
            nst_ref[:, cols1] = u_c
        act = (_gelu_tanh(conv[0]) * conv[1]).astype(BF16)
        acc = acc + _dot(act, wdown_ref[c * FF_CHUNK:(c + 1) * FF_CHUNK, :])
    y_ref[...] = h + _rms(acc, gpost2_ref[...])


def _sample_tail(x, attn, state, w_in, b_in, ln_g, ln_b, ws0, bs0, g_att, g_gm, w_out, g_pre, g_post,
                 g_pre2, g_post2, w_up, conv_w, conv_b, w_down):
    S = x.shape[0]
    args = (x, attn, state, w_in, b_in, ln_g, ln_b, ws0, bs0, g_att, g_gm, w_out, g_pre, g_post,
            g_pre2, g_post2, w_up, conv_w, conv_b, w_down)
    return pl.pallas_call(
        _sample_tail_kernel,
        grid=(1,),
        in_specs=[_const_spec(a.shape) for a in args],
        out_specs=[
            pl.BlockSpec((S, D_MODEL), lambda i: (0, 0)),
            pl.BlockSpec((S, GMLP_W), lambda i: (0, 0)),
            pl.BlockSpec((S, 4 * D_FF), lambda i: (0, 0)),
        ],
        out_shape=[
            jax.ShapeDtypeStruct((S, D_MODEL), F32),
            jax.ShapeDtypeStruct((S, GMLP_W), F32),
            jax.ShapeDtypeStruct((S, 4 * D_FF), F32),
        ],
        compiler_params=pltpu.CompilerParams(
            dimension_semantics=("arbitrary",),
            vmem_limit_bytes=VMEM_LIMIT_BYTES),
        name="sample_tail",
    )(*args)


def kernel(x_prompt, x_sample, cache_win_k, cache_win_v, state_ffn_conv, rel_bias, w_in, b_in, attn_sinks, gmlp_ln_g, gmlp_ln_b, gmlp_w_s, gmlp_b_s, g_attn_out, g_gmlp_out, w_out, g_pre_mix, g_post_mix, g_pre_ffn, g_post_ffn, w_up, ffn_conv_w, ffn_conv_b, w_down):
    depth = w_in.shape[0]
    assert depth == 1, "single-layer step"
    B, L, _ = x_prompt.shape
    S = x_sample.shape[0]
    assert x_sample.shape[1] == 1 and L % PROMPT_TILE == 0 and S % SAMPLE_GROUP == 0
    assert cache_win_k.shape[2] == BLOCK

    row = lambda a: a.reshape(1, -1)
    w_in_b = w_in[0].astype(BF16)
    w_out_b = w_out[0].astype(BF16)
    b_in_r = row(b_in[0])
    w_up_b = w_up[0].astype(BF16)
    w_down_b = w_down[0].astype(BF16)
    conv_w, conv_b = ffn_conv_w[0], row(ffn_conv_b[0])
    bs_full = jnp.repeat(gmlp_b_s[0].T, HEAD_DIM, axis=1)
    ln_g, ln_b = row(gmlp_ln_g[0]), row(gmlp_ln_b[0])
    g_att, g_gm = row(g_attn_out[0]), row(g_gmlp_out[0])
    g_pre, g_post = row(g_pre_mix[0]), row(g_post_mix[0])
    g_pre2, g_post2 = row(g_pre_ffn[0]), row(g_post_ffn[0])

    y_p, wk_p, wv_p, cst_p = _prompt_layer(
        x_prompt, w_in_b, b_in_r, rel_bias, attn_sinks[0], ln_g, ln_b, gmlp_w_s[0], bs_full,
        g_att, g_gm, w_out_b, g_pre, g_post, w_up_b, conv_w, conv_b, w_down_b, g_pre2, g_post2)
    conv_p = cst_p[:, SUBLANES - 2:, :].reshape(1, B, 2, 2 * D_FF)

    xs = x_sample.reshape(S, D_MODEL)
    attn_s, wk_s, wv_s = _sample_attention(
        xs, w_in_b, b_in_r, g_pre,
        cache_win_k[0].reshape(S, BLOCK, KV_W), cache_win_v[0].reshape(S, BLOCK, KV_W),
        rel_bias.T, attn_sinks[0].reshape(N_HEADS, 1))
    ws0 = row(jnp.repeat(gmlp_w_s[0][:, 0, 0], HEAD_DIM))
    bs0 = row(jnp.repeat(gmlp_b_s[0][:, 0], HEAD_DIM))
    y_s, vn_s, nst_s = _sample_tail(
        xs, attn_s, state_ffn_conv[0].reshape(S, 4 * D_FF), w_in_b, b_in_r,
        ln_g, ln_b, ws0, bs0, g_att, g_gm, w_out_b, g_pre, g_post, g_pre2, g_post2,
        w_up_b, conv_w, conv_b, w_down_b)

    kv_shape_p = (1, B, BLOCK, N_KV, HEAD_DIM)
    kv_shape_s = (1, S, BLOCK, N_KV, HEAD_DIM)
    return (y_p,
            y_s.reshape(S, 1, D_MODEL),
            wk_p.reshape(kv_shape_p), wv_p.reshape(kv_shape_p),
            wk_s.reshape(kv_shape_s), wv_s.reshape(kv_shape_s),
            vn_s.reshape(1, S, 1, GMLP_W),
            conv_p,
            nst_s.reshape(1, S, 2, 2 * D_FF))
```

```python
import functools
import math

import numpy as np
import jax
import jax.numpy as jnp
from jax import lax
from jax.experimental import pallas as pl
from jax.experimental.pallas import tpu as pltpu

D_MODEL = 1024
HEAD_DIM = 64
N_HEADS = 8
N_KV = 2
N_PAIRS = N_HEADS // 2
BLOCK = 128
ATTN_W = N_HEADS * HEAD_DIM
KV_W = N_KV * HEAD_DIM
GMLP_W = 512
IN_W = ATTN_W + 2 * KV_W + 2 * GMLP_W
QKV_W = ATTN_W + 2 * KV_W
D_FF = 2816
FF_CHUNK = 256
N_FF_CHUNKS = D_FF // FF_CHUNK
NUM_BUCKETS = 32
MAX_DISTANCE = 128
EPS = 1e-6
NEG_INF = -1e30
SQRT_HALF = np.sqrt(0.5).astype(np.float32)
SQRT_2_OVER_PI = np.sqrt(2 / np.pi).astype(np.float32)

LANES = 128
SUBLANES = 8
VMEM_LIMIT_BYTES = 58 * 1024 * 1024

PROMPT_TILE = 512
SAMPLE_GROUP = 16

F32 = jnp.float32
BF16 = jnp.bfloat16


def _t5_bucket_np(dist):
    n = np.maximum(dist, 0)
    max_exact = NUM_BUCKETS // 2
    nf = np.maximum(n, 1).astype(np.float32)
    large = max_exact + (np.log(nf / max_exact) / math.log(MAX_DISTANCE / max_exact)
                         * (NUM_BUCKETS - max_exact)).astype(np.int32)
    large = np.minimum(large, NUM_BUCKETS - 1)
    return np.where(n < max_exact, n, large).astype(np.int32)


def _prompt_bucket_tile():
    r = np.arange(BLOCK)[:, None]
    c = np.arange(2 * BLOCK)[None, :]
    dist = r + BLOCK - c
    ok = (dist >= 0) & (dist <= BLOCK)
    return np.where(ok, _t5_bucket_np(dist), -1).astype(np.int32)


def _sample_bucket_row():
    c = np.arange(2 * BLOCK)[None, :]
    dist = BLOCK - c
    ok = dist >= 0
    return np.where(ok, _t5_bucket_np(dist), -1).astype(np.int32)


def _rms(x, g):
    ms = jnp.mean(x * x, axis=-1, keepdims=True)
    return x * lax.rsqrt(ms + EPS) * g


def _layer_norm(x, g, b):
    mu = jnp.mean(x, axis=-1, keepdims=True)
    xc = x - mu
    y = xc * lax.rsqrt(jnp.mean(xc * xc, axis=-1, keepdims=True) + EPS)
    return y * g + b


def _gelu_erf(x):
    return 0.5 * x * (1.0 + lax.erf(x * SQRT_HALF))


def _gelu_tanh(x):
    cdf = 0.5 * (1.0 + jnp.tanh(SQRT_2_OVER_PI * (x + 0.044715 * (x * x * x))))
    return x * cdf


def _softmax_with_sink(s, sink):
    m = jnp.maximum(jnp.max(s, axis=-1, keepdims=True), sink)
    p = jnp.exp(s - m)
    l = jnp.sum(p, axis=-1, keepdims=True) + jnp.exp(sink - m)
    return p, l


def _dot(a, b):
    return jnp.dot(a, b, preferred_element_type=F32)


def _dot_nt(a, b):
    return lax.dot_general(a, b, (((1,), (1,)), ((), ())), preferred_element_type=F32)


def _shift_rows(u, prev8, n):
    s = pltpu.roll(u, n, axis=0)
    top = jnp.where(lax.broadcasted_iota(jnp.int32, prev8.shape, 0) < n,
                    pltpu.roll(prev8, n, axis=0), s[0:SUBLANES])
    return jnp.concatenate([top, s[SUBLANES:]], axis=0)


def _ff_cols(gv, c):
    start = gv * D_FF + c * FF_CHUNK
    return slice(start, start + FF_CHUNK)


def _const_spec(shape):
    nd = len(shape)
    return pl.BlockSpec(shape, lambda *_: (0,) * nd, pipeline_mode=pl.Buffered(1))


def _smem_spec():
    return pl.BlockSpec(memory_space=pltpu.SMEM)


PROJ_PIECE = 256
N_PROJ_PIECES = IN_W // PROJ_PIECE
DOWN_PIECE = 256
N_DOWN_PIECES = D_MODEL // DOWN_PIECE
OUT_PIECE = 512
N_OUT_PIECES = D_MODEL // OUT_PIECE


def _prompt_kernel(x_ref, win_ref, bin_ref, relb_ref, sink_ref, bucket_ref, lng_ref, lnb_ref,
                   ws_ref, bs_ref, gatt_ref, ggm_ref, wout_ref, gpre_ref, gpost_ref,
                   wup_ref, cw_ref, cb_ref, wdown_ref, gpre2_ref, gpost2_ref,
                   y_ref, wk_ref, wv_ref, cst_ref,
                   bias_s, xb_s, proj_s, q_s, kv_s, vnp_s, wsp_s, gmix_s, mix_s, h_s, h2_s, act_s, carry_s,
                   *, tile, n_tiles, n_total):
    step = pl.program_id(0)
    n_blk = tile // BLOCK
    t_mix = jnp.minimum(step, n_total - 1) % n_tiles
    t_ffn = jnp.maximum(step - 1, 0) % n_tiles
    slot = step % 2

    @pl.when(step == 0)
    def _init():
        bt = bucket_ref[...]
        for h in range(N_HEADS):
            acc = jnp.zeros((BLOCK, 2 * BLOCK), F32)
            for bk in range(NUM_BUCKETS):
                acc = jnp.where(bt == bk, relb_ref[bk, h], acc)
            bias_s[h] = acc
        row = lax.broadcasted_iota(jnp.int32, (BLOCK, BLOCK), 0)
        col = lax.broadcasted_iota(jnp.int32, (BLOCK, BLOCK), 1)
        tril = (row >= col).astype(F32)
        for p in range(N_PAIRS):
            wsp_s[p] = jnp.concatenate(
                [ws_ref[2 * p] * tril, ws_ref[2 * p + 1] * tril], axis=1).astype(BF16)
        kv_s[...] = jnp.zeros(kv_s.shape, BF16)
        carry_s[...] = jnp.zeros(carry_s.shape, F32)
        h_s[...] = jnp.zeros(h_s.shape, F32)

    held = {}

    def m_norm():
        prev = kv_s[:, tile:tile + BLOCK, :]
        kv_s[:, 0:BLOCK, :] = jnp.where(t_mix > 0, prev, jnp.zeros_like(prev))
        xb_s[...] = _rms(x_ref[0], gpre_ref[...]).astype(BF16)

    def m_proj(i):
        cols = slice(i * PROJ_PIECE, (i + 1) * PROJ_PIECE)
        proj_s[:, cols] = _dot(xb_s[...], win_ref[:, cols]) + bin_ref[:, cols]

    def m_q():
        q_s[...] = (proj_s[:, :ATTN_W] * (HEAD_DIM ** -0.5)).astype(BF16)

    def m_kv():
        k = proj_s[:, ATTN_W:ATTN_W + KV_W]
        v = proj_s[:, ATTN_W + KV_W:QKV_W]
        wk_ref[0] = k[tile - BLOCK:, :]
        wv_ref[0] = v[tile - BLOCK:, :]
        lo = lax.broadcasted_iota(jnp.int32, (tile, LANES), 1) < HEAD_DIM
        for base, val in ((0, k), (4, v)):
            rolled = pltpu.roll(val, HEAD_DIM, axis=1)
            kv_s[base + 0, BLOCK:, :] = jnp.where(lo, val, 0.0).astype(BF16)
            kv_s[base + 1, BLOCK:, :] = jnp.where(lo, 0.0, rolled).astype(BF16)
            kv_s[base + 2, BLOCK:, :] = jnp.where(lo, rolled, 0.0).astype(BF16)
            kv_s[base + 3, BLOCK:, :] = jnp.where(lo, 0.0, val).astype(BF16)

    def m_u():
        ucols = slice(QKV_W, QKV_W + GMLP_W)
        proj_s[:, ucols] = _gelu_erf(proj_s[:, ucols])

    def m_vg():
        vn = _layer_norm(_gelu_erf(proj_s[:, QKV_W + GMLP_W:]), lng_ref[...], lnb_ref[...])
        lo4 = (lax.broadcasted_iota(jnp.int32, (tile, GMLP_W), 1) & (LANES - 1)) < HEAD_DIM
        vnp_s[0] = jnp.where(lo4, vn, 0.0).astype(BF16)
        vnp_s[1] = jnp.where(lo4, 0.0, vn).astype(BF16)

    def m_gmlp():
        for p in range(N_PAIRS):
            cols = slice(p * LANES, (p + 1) * LANES)
            rhs = jnp.concatenate(
                [jnp.concatenate([vnp_s[0, j * BLOCK:(j + 1) * BLOCK, cols],
                                  vnp_s[1, j * BLOCK:(j + 1) * BLOCK, cols]], axis=0)
                 for j in range(n_blk)], axis=1)
            gmix_s[p] = _dot(wsp_s[p], rhs)

    def m_block(j):
        r = lax.broadcasted_iota(jnp.int32, (BLOCK, 2 * BLOCK), 0)
        c = lax.broadcasted_iota(jnp.int32, (BLOCK, 2 * BLOCK), 1)
        dist = r + BLOCK - c
        allowed = (dist >= 0) & (dist <= BLOCK)
        if j == 0:
            allowed = allowed & (c >= jnp.where(t_mix > 0, 0, BLOCK))
        lane_lo = lax.broadcasted_iota(jnp.int32, (BLOCK, LANES), 1) < HEAD_DIM
        rows = slice(j * BLOCK, (j + 1) * BLOCK)
        krows = slice(j * BLOCK, (j + 2) * BLOCK)

        attn_parts = []
        for g in range(N_KV):
            pairs = (2 * g, 2 * g + 1)
            qg = jnp.concatenate([q_s[rows, p * LANES:(p + 1) * LANES] for p in pairs], axis=0)
            kcat = jnp.concatenate([kv_s[2 * g, krows, :], kv_s[2 * g + 1, krows, :]], axis=0)
            s = _dot_nt(qg, kcat)
            probs, denoms = [], []
            for i, p in enumerate(pairs):
                pair_probs = []
                for par in range(2):
                    h = 2 * p + par
                    sh = s[i * BLOCK:(i + 1) * BLOCK, par * 2 * BLOCK:(par + 1) * 2 * BLOCK]
                    sh = jnp.where(allowed, sh + bias_s[h], NEG_INF)
                    ph, lh = _softmax_with_sink(sh, sink_ref[h])
                    pair_probs.append(ph.astype(BF16))
                    denoms.append(lh)
                probs.append(jnp.concatenate(pair_probs, axis=1))
            vcat = jnp.concatenate([kv_s[4 + 2 * g, krows, :], kv_s[4 + 2 * g + 1, krows, :]], axis=0)
            o = _dot(jnp.concatenate(probs, axis=0), vcat)
            for i in range(2):
                inv = jnp.where(lane_lo, 1.0 / denoms[2 * i], 1.0 / denoms[2 * i + 1])
                attn_parts.append(o[i * BLOCK:(i + 1) * BLOCK] * inv)

        gm_parts = []
        for p in range(N_PAIRS):
            cols = slice(p * LANES, (p + 1) * LANES)
            mixed = gmix_s[p, :, rows] + bs_ref[:, cols]
            ucols = slice(QKV_W + p * LANES, QKV_W + (p + 1) * LANES)
            gm_parts.append(proj_s[rows, ucols] * mixed)

        attn = jnp.concatenate(attn_parts, axis=1)
        gm = jnp.concatenate(gm_parts, axis=1)
        mix_s[rows, 0:ATTN_W] = _rms(attn, gatt_ref[...]).astype(BF16)
        mix_s[rows, ATTN_W:] = _rms(gm, ggm_ref[...]).astype(BF16)

    def m_out(n):
        held["out", n] = _dot(mix_s[...], wout_ref[:, n * OUT_PIECE:(n + 1) * OUT_PIECE])

    def m_res():
        mixed = jnp.concatenate([held.pop(("out", n)) for n in range(N_OUT_PIECES)], axis=1)
        h_s[slot] = x_ref[0] + _rms(mixed, gpost_ref[...])

    def f_norm():
        h2_s[...] = _rms(h_s[1 - slot], gpre2_ref[...]).astype(BF16)

    def f_up(c):
        w = jnp.concatenate([wup_ref[:, _ff_cols(gv, c)] for gv in range(2)], axis=1)
        held["up", c] = _dot(h2_s[...], w)

    def f_gate(c):
        conv = []
        up = held.pop(("up", c))
        for gv in range(2):
            u = up[:, gv * FF_CHUNK:(gv + 1) * FF_CHUNK]
            cols = _ff_cols(gv, c)
            prev8 = jnp.where(t_ffn > 0, carry_s[:, cols], 0.0)
            conv.append(cb_ref[:, cols] + cw_ref[0:1, cols] * _shift_rows(u, prev8, 2)
                        + cw_ref[1:2, cols] * _shift_rows(u, prev8, 1) + cw_ref[2:3, cols] * u)
            carry_s[:, cols] = u[tile - SUBLANES:, :]
        act_s[:, c * FF_CHUNK:(c + 1) * FF_CHUNK] = (_gelu_tanh(conv[0]) * conv[1]).astype(BF16)

    def f_down(n):
        held["down", n] = _dot(act_s[...], wdown_ref[:, n * DOWN_PIECE:(n + 1) * DOWN_PIECE])

    def f_res():
        f = jnp.concatenate([held.pop(("down", n)) for n in range(N_DOWN_PIECES)], axis=1)
        cst_ref[0] = carry_s[...]
        y_ref[0] = h_s[1 - slot] + _rms(f, gpost2_ref[...])

    assert n_blk == N_DOWN_PIECES and N_PROJ_PIECES <= N_FF_CHUNKS
    q0, q1, kv, u0, u1, vg0, vg1 = range(N_PROJ_PIECES)
    proj_order = [vg0, vg1, u0, u1, kv, q0, q1]
    slot_vector = {2: [m_vg], 3: [m_gmlp], 4: [m_u], 5: [m_kv], 7: [m_q]}
    f_norm()
    m_norm()
    f_up(0)
    for c in range(N_FF_CHUNKS):
        if c + 1 < N_FF_CHUNKS:
            f_up(c + 1)
        if c < len(proj_order):
            m_proj(proj_order[c])
        f_gate(c)
        for item in slot_vector.get(c, []):
            item()
    f_down(0)
    for j in range(n_blk):
        if j + 1 < N_DOWN_PIECES:
            f_down(j + 1)
        m_block(j)
    m_out(0)
    f_res()
    for n in range(1, N_OUT_PIECES):
        m_out(n)
    m_res()


def _prompt_layer(x, w_in, b_in, rel_bias, sinks, ln_g, ln_b, w_s, bs_full, g_att, g_gm, w_out,
                  g_pre, g_post, w_up, conv_w, conv_b, w_down, g_pre2, g_post2):
    B, L, _ = x.shape
    tile = PROMPT_TILE
    n_tiles = L // tile
    n_total = B * n_tiles
    bucket = jnp.asarray(_prompt_bucket_tile())

    def mix_tile(s):
        return jnp.minimum(s, n_total - 1)

    def ffn_tile(s):
        return jnp.maximum(s - 1, 0)

    kern = functools.partial(_prompt_kernel, tile=tile, n_tiles=n_tiles, n_total=n_total)
    return pl.pallas_call(
        kern,
        grid=(n_total + 1,),
        in_specs=[
            pl.BlockSpec((1, tile, D_MODEL), lambda s: (mix_tile(s) // n_tiles, mix_tile(s) % n_tiles, 0)),
            _const_spec((D_MODEL, IN_W)),
            _const_spec((1, IN_W)),
            _smem_spec(),
            _smem_spec(),
            _const_spec((BLOCK, 2 * BLOCK)),
            _const_spec((1, GMLP_W)),
            _const_spec((1, GMLP_W)),
            _const_spec((N_HEADS, BLOCK, BLOCK)),
            _const_spec((BLOCK, GMLP_W)),
            _const_spec((1, ATTN_W)),
            _const_spec((1, GMLP_W)),
            _const_spec((D_MODEL, D_MODEL)),
            _const_spec((1, D_MODEL)),
            _const_spec((1, D_MODEL)),
            _const_spec((D_MODEL, 2 * D_FF)),
            _const_spec((3, 2 * D_FF)),
            _const_spec((1, 2 * D_FF)),
            _const_spec((D_FF, D_MODEL)),
            _const_spec((1, D_MODEL)),
            _const_spec((1, D_MODEL)),
        ],
        out_specs=[
            pl.BlockSpec((1, tile, D_MODEL), lambda s: (ffn_tile(s) // n_tiles, ffn_tile(s) % n_tiles, 0)),
            pl.BlockSpec((1, BLOCK, KV_W), lambda s: (mix_tile(s) // n_tiles, 0, 0)),
            pl.BlockSpec((1, BLOCK, KV_W), lambda s: (mix_tile(s) // n_tiles, 0, 0)),
            pl.BlockSpec((1, SUBLANES, 2 * D_FF), lambda s: (ffn_tile(s) // n_tiles, 0, 0)),
        ],
        out_shape=[
            jax.ShapeDtypeStruct((B, L, D_MODEL), F32),
            jax.ShapeDtypeStruct((B, BLOCK, KV_W), F32),
            jax.ShapeDtypeStruct((B, BLOCK, KV_W), F32),
            jax.ShapeDtypeStruct((B, SUBLANES, 2 * D_FF), F32),
        ],
        scratch_shapes=[
            pltpu.VMEM((N_HEADS, BLOCK, 2 * BLOCK), F32),
            pltpu.VMEM((tile, D_MODEL), BF16),
            pltpu.VMEM((tile, IN_W), F32),
            pltpu.VMEM((tile, ATTN_W), BF16),
            pltpu.VMEM((8, tile + BLOCK, LANES), BF16),
            pltpu.VMEM((2, tile, GMLP_W), BF16),
            pltpu.VMEM((N_PAIRS, BLOCK, 2 * BLOCK), BF16),
            pltpu.VMEM((N_PAIRS, BLOCK, tile), F32),
            pltpu.VMEM((tile, D_MODEL), BF16),
            pltpu.VMEM((2, tile, D_MODEL), F32),
            pltpu.VMEM((tile, D_MODEL), BF16),
            pltpu.VMEM((tile, D_FF), BF16),
            pltpu.VMEM((SUBLANES, 2 * D_FF), F32),
        ],
        compiler_params=pltpu.CompilerParams(
            dimension_semantics=("arbitrary",),
            vmem_limit_bytes=VMEM_LIMIT_BYTES),
        name="prompt_layer",
    )(x, w_in, b_in, rel_bias, sinks, bucket, ln_g, ln_b, w_s, bs_full, g_att, g_gm, w_out,
      g_pre, g_post, w_up, conv_w, conv_b, w_down, g_pre2, g_post2)


def _sample_attn_kernel(x_ref, wqkv_ref, bqkv_ref, gpre_ref, ck_ref, cv_ref, relbt_ref, sink_ref,
                        bucket_ref, attn_ref, wk_ref, wv_ref,
                        bias_s, kext_s, vext_s, rep_s, *, group):
    i = pl.program_id(0)
    rows = group * SUBLANES

    @pl.when(i == 0)
    def _init():
        bt = jnp.broadcast_to(bucket_ref[...], (N_HEADS, 2 * BLOCK))
        acc = jnp.zeros((N_HEADS, 2 * BLOCK), F32)
        for bk in range(NUM_BUCKETS):
            acc = jnp.where(bt == bk, relbt_ref[:, bk:bk + 1], acc)
        bias_s[...] = acc
        kext_s[...] = jnp.zeros(kext_s.shape, F32)
        vext_s[...] = jnp.zeros(vext_s.shape, F32)

    xn = _rms(x_ref[...], gpre_ref[...])
    xr = jnp.concatenate(
        [jnp.broadcast_to(xn[s:s + 1, :], (SUBLANES, D_MODEL)) for s in range(group)], axis=0)
    proj = _dot(xr.astype(BF16), wqkv_ref[...]) + bqkv_ref[...]
    q = proj[:, :ATTN_W] * (HEAD_DIM ** -0.5)
    k3 = proj[:, ATTN_W:ATTN_W + KV_W].reshape(group, SUBLANES, LANES)
    v3 = proj[:, ATTN_W + KV_W:].reshape(group, SUBLANES, LANES)

    head = lax.broadcasted_iota(jnp.int32, (rows, LANES), 0) & (SUBLANES - 1)
    lane_half = lax.broadcasted_iota(jnp.int32, (rows, LANES), 1) // HEAD_DIM
    lane_lo = lane_half == 0
    qpad = jnp.zeros((rows, LANES), F32)
    for p in range(N_PAIRS):
        g = p // 2
        qp = q[:, p * LANES:(p + 1) * LANES]
        qr = pltpu.roll(qp, HEAD_DIM, axis=1)
        src_even = qp if g == 0 else qr
        src_odd = qr if g == 0 else qp
        qpad = jnp.where((head == 2 * p) & (lane_half == g), src_even, qpad)
        qpad = jnp.where((head == 2 * p + 1) & (lane_half == g), src_odd, qpad)
    qpad = qpad.reshape(group, SUBLANES, LANES).astype(BF16)

    ck = ck_ref[...]
    cv = cv_ref[...]
    kext_s[:, 0:BLOCK, :] = ck
    vext_s[:, 0:BLOCK, :] = cv
    kext_s[:, BLOCK:BLOCK + SUBLANES, :] = k3
    vext_s[:, BLOCK:BLOCK + SUBLANES, :] = v3

    s = jnp.einsum('shc,sjc->shj', qpad, kext_s[...].astype(BF16), preferred_element_type=F32)
    allowed = jnp.broadcast_to(bucket_ref[...], (N_HEADS, 2 * BLOCK)) >= 0
    s = jnp.where(allowed[None], s + bias_s[...][None], NEG_INF)
    pr, l = _softmax_with_sink(s, sink_ref[...][None])
    o = jnp.einsum('shj,sjc->shc', pr.astype(BF16), vext_s[...].astype(BF16),
                   preferred_element_type=F32)
    o = (o * (1.0 / l)).reshape(rows, LANES)

    om = jnp.where(lane_half == head // (N_HEADS // N_KV), o, 0.0)
    both = (om + pltpu.roll(om, HEAD_DIM, axis=1)).reshape(group, SUBLANES, LANES)
    lane_lo3 = lane_lo.reshape(group, SUBLANES, LANES)
    for p in range(N_PAIRS):
        even = jnp.broadcast_to(both[:, 2 * p:2 * p + 1, :], (group, SUBLANES, LANES))
        odd = jnp.broadcast_to(both[:, 2 * p + 1:2 * p + 2, :], (group, SUBLANES, LANES))
        rep_s[p] = jnp.where(lane_lo3, even, odd).reshape(rows, LANES)
    for p in range(N_PAIRS):
        attn_ref[:, p * LANES:(p + 1) * LANES] = rep_s[p, pl.ds(0, group, stride=SUBLANES), :]

    last = lax.broadcasted_iota(jnp.int32, (group, SUBLANES, LANES), 1) == SUBLANES - 1
    for cache, new3, out_ref in ((ck, k3, wk_ref), (cv, v3, wv_ref)):
        shifted = pltpu.roll(cache, BLOCK - 1, axis=1)
        out_ref[:, 0:BLOCK - SUBLANES, :] = shifted[:, 0:BLOCK - SUBLANES, :]
        out_ref[:, BLOCK - SUBLANES:, :] = jnp.where(last, new3, shifted[:, BLOCK - SUBLANES:, :])


def _sample_attention(x, wqkv, bqkv, g_pre, cache_k, cache_v, relb_t, sinks_col):
    S = x.shape[0]
    group = SAMPLE_GROUP
    bucket = jnp.asarray(_sample_bucket_row())
    kern = functools.partial(_sample_attn_kernel, group=group)
    return pl.pallas_call(
        kern,
        grid=(S // group,),
        in_specs=[
            pl.BlockSpec((group, D_MODEL), lambda i: (i, 0)),
            _const_spec((D_MODEL, QKV_W)),
            _const_spec((1, QKV_W)),
            _const_spec((1, D_MODEL)),
            pl.BlockSpec((group, BLOCK, KV_W), lambda i: (i, 0, 0)),
            pl.BlockSpec((group, BLOCK, KV_W), lambda i: (i, 0, 0)),
            _const_spec((N_HEADS, NUM_BUCKETS)),
            _const_spec((N_HEADS, 1)),
            _const_spec((1, 2 * BLOCK)),
        ],
        out_specs=[
            pl.BlockSpec((group, ATTN_W), lambda i: (i, 0)),
            pl.BlockSpec((group, BLOCK, KV_W), lambda i: (i, 0, 0)),
            pl.BlockSpec((group, BLOCK, KV_W), lambda i: (i, 0, 0)),
        ],
        out_shape=[
            jax.ShapeDtypeStruct((S, ATTN_W), F32),
            jax.ShapeDtypeStruct((S, BLOCK, KV_W), F32),
            jax.ShapeDtypeStruct((S, BLOCK, KV_W), F32),
        ],
        scratch_shapes=[
            pltpu.VMEM((N_HEADS, 2 * BLOCK), F32),
            pltpu.VMEM((group, 2 * BLOCK, LANES), F32),
            pltpu.VMEM((group, 2 * BLOCK, LANES), F32),
            pltpu.VMEM((N_PAIRS, group * SUBLANES, LANES), F32),
        ],
        compiler_params=pltpu.CompilerParams(
            dimension_semantics=("arbitrary",),
            vmem_limit_bytes=VMEM_LIMIT_BYTES),
        name="sample_attention",
    )(x, wqkv, bqkv, g_pre, cache_k, cache_v, relb_t, sinks_col, bucket)


def _sample_tail_kernel(x_ref, attn_ref, st_ref, win_ref, bin_ref, lng_ref, lnb_ref, ws0_ref, bs0_ref,
                        gatt_ref, ggm_ref, wout_ref, gpre_ref, gpost_ref, gpre2_ref, gpost2_ref,
                        wup_ref, cw_ref, cb_ref, wdown_ref,
                        y_ref, vn_ref, nst_ref):
    x = x_ref[...]
    xn = _rms(x, gpre_ref[...])
    puv = _dot(xn.astype(BF16), win_ref[:, QKV_W:]) + bin_ref[:, QKV_W:]
    u = _gelu_erf(puv[:, :GMLP_W])
    vn = _layer_norm(_gelu_erf(puv[:, GMLP_W:]), lng_ref[...], lnb_ref[...])
    vn_ref[...] = vn
    gm = u * (ws0_ref[...] * vn + bs0_ref[...])
    mix = jnp.concatenate([_rms(attn_ref[...], gatt_ref[...]), _rms(gm, ggm_ref[...])], axis=1)
    h = x + _rms(_dot(mix.astype(BF16), wout_ref[...]), gpost_ref[...])

    h2 = _rms(h, gpre2_ref[...]).astype(BF16)
    two_ff = 2 * D_FF
    nst_ref[:, 0:two_ff] = st_ref[:, two_ff:]
    acc = jnp.zeros((x.shape[0], D_MODEL), F32)
    for c in range(N_FF_CHUNKS):
        conv = []
        for gv in range(2):
            cols = _ff_cols(gv, c)
            cols1 = slice(two_ff + cols.start, two_ff + cols.stop)
            u_c = _dot(h2, wup_ref[:, cols])
            conv.append(cb_ref[:, cols] + cw_ref[0:1, cols] * st_ref[:, cols]
                        + cw_ref[1:2, cols] * st_ref[:, cols1] + cw_ref[2:3, cols] * u_c)
            nst_ref[:, cols1] = u_c
        act = (_gelu_tanh(conv[0]) * conv[1]).astype(BF16)
        acc = acc + _dot(act, wdown_ref[c * FF_CHUNK:(c + 1) * FF_CHUNK, :])
    y_ref[...] = h + _rms(acc, gpost2_ref[...])


def _sample_tail(x, attn, state, w_in, b_in, ln_g, ln_b, ws0, bs0, g_att, g_gm, w_out, g_pre, g_post,
                 g_pre2, g_post2, w_up, conv_w, conv_b, w_down):
    S = x.shape[0]
    args = (x, attn, state, w_in, b_in, ln_g, ln_b, ws0, bs0, g_att, g_gm, w_out, g_pre, g_post,
            g_pre2, g_post2, w_up, conv_w, conv_b, w_down)
    return pl.pallas_call(
        _sample_tail_kernel,
        grid=(1,),
        in_specs=[_const_spec(a.shape) for a in args],
        out_specs=[
            pl.BlockSpec((S, D_MODEL), lambda i: (0, 0)),
            pl.BlockSpec((S, GMLP_W), lambda i: (0, 0)),
            pl.BlockSpec((S, 4 * D_FF), lambda i: (0, 0)),
        ],
        out_shape=[
            jax.ShapeDtypeStruct((S, D_MODEL), F32),
            jax.ShapeDtypeStruct((S, GMLP_W), F32),
            jax.ShapeDtypeStruct((S, 4 * D_FF), F32),
        ],
        compiler_params=pltpu.CompilerParams(
            dimension_semantics=("arbitrary",),
            vmem_limit_bytes=VMEM_LIMIT_BYTES),
        name="sample_tail",
    )(*args)


def kernel(x_prompt, x_sample, cache_win_k, cache_win_v, state_ffn_conv, rel_bias, w_in, b_in, attn_sinks, gmlp_ln_g, gmlp_ln_b, gmlp_w_s, gmlp_b_s, g_attn_out, g_gmlp_out, w_out, g_pre_mix, g_post_mix, g_pre_ffn, g_post_ffn, w_up, ffn_conv_w, ffn_conv_b, w_down):
    depth = w_in.shape[0]
    assert depth == 1, "single-layer step"
    B, L, _ = x_prompt.shape
    S = x_sample.shape[0]
    assert x_sample.shape[1] == 1 and L % PROMPT_TILE == 0 and S % SAMPLE_GROUP == 0
    assert cache_win_k.shape[2] == BLOCK

    row = lambda a: a.reshape(1, -1)
    w_in_b = w_in[0].astype(BF16)
    w_out_b = w_out[0].astype(BF16)
    b_in_r = row(b_in[0])
    w_up_b = w_up[0].astype(BF16)
    w_down_b = w_down[0].astype(BF16)
    conv_w, conv_b = ffn_conv_w[0], row(ffn_conv_b[0])
    bs_full = jnp.repeat(gmlp_b_s[0].T, HEAD_DIM, axis=1)
    ln_g, ln_b = row(gmlp_ln_g[0]), row(gmlp_ln_b[0])
    g_att, g_gm = row(g_attn_out[0]), row(g_gmlp_out[0])
    g_pre, g_post = row(g_pre_mix[0]), row(g_post_mix[0])
    g_pre2, g_post2 = row(g_pre_ffn[0]), row(g_post_ffn[0])

    y_p, wk_p, wv_p, cst_p = _prompt_layer(
        x_prompt, w_in_b, b_in_r, rel_bias, attn_sinks[0], ln_g, ln_b, gmlp_w_s[0], bs_full,
        g_att, g_gm, w_out_b, g_pre, g_post, w_up_b, conv_w, conv_b, w_down_b, g_pre2, g_post2)
    conv_p = cst_p[:, SUBLANES - 2:, :].reshape(1, B, 2, 2 * D_FF)

    xs = x_sample.reshape(S, D_MODEL)
    attn_s, wk_s, wv_s = _sample_attention(
        xs, w_in_b, b_in_r, g_pre,
        cache_win_k[0].reshape(S, BLOCK, KV_W), cache_win_v[0].reshape(S, BLOCK, KV_W),
        rel_bias.T, attn_sinks[0].reshape(N_HEADS, 1))
    ws0 = row(jnp.repeat(gmlp_w_s[0][:, 0, 0], HEAD_DIM))
    bs0 = row(jnp.repeat(gmlp_b_s[0][:, 0], HEAD_DIM))
    y_s, vn_s, nst_s = _sample_tail(
        xs, attn_s, state_ffn_conv[0].reshape(S, 4 * D_FF), w_in_b, b_in_r,
        ln_g, ln_b, ws0, bs0, g_att, g_gm, w_out_b, g_pre, g_post, g_pre2, g_post2,
        w_up_b, conv_w, conv_b, w_down_b)

    kv_shape_p = (1, B, BLOCK, N_KV, HEAD_DIM)
    kv_shape_s = (1, S, BLOCK, N_KV, HEAD_DIM)
    return (y_p,
            y_s.reshape(S, 1, D_MODEL),
            wk_p.reshape(kv_shape_p), wv_p.reshape(kv_shape_p),
            wk_s.reshape(kv_shape_s), wv_s.reshape(kv_shape_s),
            vn_s.reshape(1, S, 1, GMLP_W),
            conv_p,
            nst_s.reshape(1, S, 2, 2 * D_FF))
```

```python
import functools
import math

import numpy as np
import jax
import jax.numpy as jnp
from jax import lax
from jax.experimental import pallas as pl
from jax.experimental.pallas import tpu as pltpu

D_MODEL = 1024
HEAD_DIM = 64
N_HEADS = 8
N_KV = 2
N_PAIRS = N_HEADS // 2
BLOCK = 128
ATTN_W = N_HEADS * HEAD_DIM
KV_W = N_KV * HEAD_DIM
GMLP_W = 512
IN_W = ATTN_W + 2 * KV_W + 2 * GMLP_W
QKV_W = ATTN_W + 2 * KV_W
D_FF = 2816
FF_CHUNK = 256
N_FF_CHUNKS = D_FF // FF_CHUNK
NUM_BUCKETS = 32
MAX_DISTANCE = 128
EPS = 1e-6
NEG_INF = -1e30
SQRT_HALF = np.sqrt(0.5).astype(np.float32)
SQRT_2_OVER_PI = np.sqrt(2 / np.pi).astype(np.float32)

LANES = 128
SUBLANES = 8
VMEM_LIMIT_BYTES = 58 * 1024 * 1024

PROMPT_TILE = 512
SAMPLE_GROUP = 16

F32 = jnp.float32
BF16 = jnp.bfloat16


def _t5_bucket_np(dist):
    n = np.maximum(dist, 0)
    max_exact = NUM_BUCKETS // 2
    nf = np.maximum(n, 1).astype(np.float32)
    large = max_exact + (np.log(nf / max_exact) / math.log(MAX_DISTANCE / max_exact)
                         * (NUM_BUCKETS - max_exact)).astype(np.int32)
    large = np.minimum(large, NUM_BUCKETS - 1)
    return np.where(n < max_exact, n, large).astype(np.int32)


def _prompt_bucket_tile():
    r = np.arange(BLOCK)[:, None]
    c = np.arange(2 * BLOCK)[None, :]
    dist = r + BLOCK - c
    ok = (dist >= 0) & (dist <= BLOCK)
    return np.where(ok, _t5_bucket_np(dist), -1).astype(np.int32)


def _sample_bucket_row():
    c = np.arange(2 * BLOCK)[None, :]
    dist = BLOCK - c
    ok = dist >= 0
    return np.where(ok, _t5_bucket_np(dist), -1).astype(np.int32)


def _rms(x, g):
    ms = jnp.mean(x * x, axis=-1, keepdims=True)
    return x * lax.rsqrt(ms + EPS) * g


def _layer_norm(x, g, b):
    mu = jnp.mean(x, axis=-1, keepdims=True)
    xc = x - mu
    y = xc * lax.rsqrt(jnp.mean(xc * xc, axis=-1, keepdims=True) + EPS)
    return y * g + b


def _gelu_erf(x):
    return 0.5 * x * (1.0 + lax.erf(x * SQRT_HALF))


def _gelu_tanh(x):
    cdf = 0.5 * (1.0 + jnp.tanh(SQRT_2_OVER_PI * (x + 0.044715 * (x * x * x))))
    return x * cdf


def _softmax_with_sink(s, sink):
    m = jnp.maximum(jnp.max(s, axis=-1, keepdims=True), sink)
    p = jnp.exp(s - m)
    l = jnp.sum(p, axis=-1, keepdims=True) + jnp.exp(sink - m)
    return p, l


def _dot(a, b):
    return jnp.dot(a, b, preferred_element_type=F32)


def _dot_nt(a, b):
    return lax.dot_general(a, b, (((1,), (1,)), ((), ())), preferred_element_type=F32)


def _shift_rows(u, prev8, n):
    s = pltpu.roll(u, n, axis=0)
    top = jnp.where(lax.broadcasted_iota(jnp.int32, prev8.shape, 0) < n,
                    pltpu.roll(prev8, n, axis=0), s[0:SUBLANES])
    return jnp.concatenate([top, s[SUBLANES:]], axis=0)


def _ff_cols(gv, c):
    start = gv * D_FF + c * FF_CHUNK
    return slice(start, start + FF_CHUNK)


def _const_spec(shape):
    nd = len(shape)
    return pl.BlockSpec(shape, lambda *_: (0,) * nd, pipeline_mode=pl.Buffered(1))


def _smem_spec():
    return pl.BlockSpec(memory_space=pltpu.SMEM)


PROJ_PIECE = 256
N_PROJ_PIECES = IN_W // PROJ_PIECE
DOWN_PIECE = 256
N_DOWN_PIECES = D_MODEL // DOWN_PIECE
OUT_ROWS = 256


def _prompt_kernel(x_ref, win_ref, bin_ref, relb_ref, sink_ref, bucket_ref, lng_ref, lnb_ref,
                   ws_ref, bs_ref, gatt_ref, ggm_ref, wout_ref, gpre_ref, gpost_ref,
                   wup_ref, cw_ref, cb_ref, wdown_ref, gpre2_ref, gpost2_ref,
                   y_ref, wk_ref, wv_ref, cst_ref,
                   bias_s, xb_s, proj_s, q_s, kv_s, vnp_s, wsp_s, gmix_s, mix_s, h_s, h2_s, act_s, carry_s,
                   *, tile, n_tiles, n_total):
    step = pl.program_id(0)
    n_blk = tile // BLOCK
    t_mix = jnp.minimum(step, n_total - 1) % n_tiles
    t_ffn = jnp.maximum(step - 1, 0) % n_tiles
    slot = step % 2

    @pl.when(step == 0)
    def _init():
        bt = bucket_ref[...]
        for h in range(N_HEADS):
            acc = jnp.zeros((BLOCK, 2 * BLOCK), F32)
            for bk in range(NUM_BUCKETS):
                acc = jnp.where(bt == bk, relb_ref[bk, h], acc)
            bias_s[h] = acc
        row = lax.broadcasted_iota(jnp.int32, (BLOCK, BLOCK), 0)
        col = lax.broadcasted_iota(jnp.int32, (BLOCK, BLOCK), 1)
        tril = (row >= col).astype(F32)
        for p in range(N_PAIRS):
            wsp_s[p] = jnp.concatenate(
                [ws_ref[2 * p] * tril, ws_ref[2 * p + 1] * tril], axis=1).astype(BF16)
        kv_s[...] = jnp.zeros(kv_s.shape, BF16)
        carry_s[...] = jnp.zeros(carry_s.shape, F32)
        h_s[...] = jnp.zeros(h_s.shape, F32)

    held = {}

    def m_norm():
        prev = kv_s[:, tile:tile + BLOCK, :]
        kv_s[:, 0:BLOCK, :] = jnp.where(t_mix > 0, prev, jnp.zeros_like(prev))
        xb_s[...] = _rms(x_ref[0], gpre_ref[...]).astype(BF16)

    def m_proj(i):
        cols = slice(i * PROJ_PIECE, (i + 1) * PROJ_PIECE)
        proj_s[:, cols] = _dot(xb_s[...], win_ref[:, cols]) + bin_ref[:, cols]

    def m_q():
        q_s[...] = (proj_s[:, :ATTN_W] * (HEAD_DIM ** -0.5)).astype(BF16)

    def m_kv():
        k = proj_s[:, ATTN_W:ATTN_W + KV_W]
        v = proj_s[:, ATTN_W + KV_W:QKV_W]
        wk_ref[0] = k[tile - BLOCK:, :]
        wv_ref[0] = v[tile - BLOCK:, :]
        lo = lax.broadcasted_iota(jnp.int32, (tile, LANES), 1) < HEAD_DIM
        for base, val in ((0, k), (4, v)):
            rolled = pltpu.roll(val, HEAD_DIM, axis=1)
            kv_s[base + 0, BLOCK:, :] = jnp.where(lo, val, 0.0).astype(BF16)
            kv_s[base + 1, BLOCK:, :] = jnp.where(lo, 0.0, rolled).astype(BF16)
            kv_s[base + 2, BLOCK:, :] = jnp.where(lo, rolled, 0.0).astype(BF16)
            kv_s[base + 3, BLOCK:, :] = jnp.where(lo, 0.0, val).astype(BF16)

    def m_u():
        ucols = slice(QKV_W, QKV_W + GMLP_W)
        proj_s[:, ucols] = _gelu_erf(proj_s[:, ucols])

    def m_vg():
        vn = _layer_norm(_gelu_erf(proj_s[:, QKV_W + GMLP_W:]), lng_ref[...], lnb_ref[...])
        lo4 = (lax.broadcasted_iota(jnp.int32, (tile, GMLP_W), 1) & (LANES - 1)) < HEAD_DIM
        vnp_s[0] = jnp.where(lo4, vn, 0.0).astype(BF16)
        vnp_s[1] = jnp.where(lo4, 0.0, vn).astype(BF16)

    def m_gmlp():
        for p in range(N_PAIRS):
            cols = slice(p * LANES, (p + 1) * LANES)
            rhs = jnp.concatenate(
                [jnp.concatenate([vnp_s[0, j * BLOCK:(j + 1) * BLOCK, cols],
                                  vnp_s[1, j * BLOCK:(j + 1) * BLOCK, cols]], axis=0)
                 for j in range(n_blk)], axis=1)
            gmix_s[p] = _dot(wsp_s[p], rhs)

    def m_block(j):
        r = lax.broadcasted_iota(jnp.int32, (BLOCK, 2 * BLOCK), 0)
        c = lax.broadcasted_iota(jnp.int32, (BLOCK, 2 * BLOCK), 1)
        dist = r + BLOCK - c
        allowed = (dist >= 0) & (dist <= BLOCK)
        if j == 0:
            allowed = allowed & (c >= jnp.where(t_mix > 0, 0, BLOCK))
        lane_lo = lax.broadcasted_iota(jnp.int32, (BLOCK, LANES), 1) < HEAD_DIM
        rows = slice(j * BLOCK, (j + 1) * BLOCK)
        krows = slice(j * BLOCK, (j + 2) * BLOCK)

        attn_parts = []
        for g in range(N_KV):
            pairs = (2 * g, 2 * g + 1)
            qg = jnp.concatenate([q_s[rows, p * LANES:(p + 1) * LANES] for p in pairs], axis=0)
            kcat = jnp.concatenate([kv_s[2 * g, krows, :], kv_s[2 * g + 1, krows, :]], axis=0)
            s = _dot_nt(qg, kcat)
            probs, denoms = [], []
            for i, p in enumerate(pairs):
                pair_probs = []
                for par in range(2):
                    h = 2 * p + par
                    sh = s[i * BLOCK:(i + 1) * BLOCK, par * 2 * BLOCK:(par + 1) * 2 * BLOCK]
                    sh = jnp.where(allowed, sh + bias_s[h], NEG_INF)
                    ph, lh = _softmax_with_sink(sh, sink_ref[h])
                    pair_probs.append(ph.astype(BF16))
                    denoms.append(lh)
                probs.append(jnp.concatenate(pair_probs, axis=1))
            vcat = jnp.concatenate([kv_s[4 + 2 * g, krows, :], kv_s[4 + 2 * g + 1, krows, :]], axis=0)
            o = _dot(jnp.concatenate(probs, axis=0), vcat)
            for i in range(2):
                inv = jnp.where(lane_lo, 1.0 / denoms[2 * i], 1.0 / denoms[2 * i + 1])
                attn_parts.append(o[i * BLOCK:(i + 1) * BLOCK] * inv)

        gm_parts = []
        for p in range(N_PAIRS):
            cols = slice(p * LANES, (p + 1) * LANES)
            mixed = gmix_s[p, :, rows] + bs_ref[:, cols]
            ucols = slice(QKV_W + p * LANES, QKV_W + (p + 1) * LANES)
            gm_parts.append(proj_s[rows, ucols] * mixed)

        attn = jnp.concatenate(attn_parts, axis=1)
        gm = jnp.concatenate(gm_parts, axis=1)
        mix_s[rows, 0:ATTN_W] = _rms(attn, gatt_ref[...]).astype(BF16)
        mix_s[rows, ATTN_W:] = _rms(gm, ggm_ref[...]).astype(BF16)

    def m_out(i):
        rows = slice(i * OUT_ROWS, (i + 1) * OUT_ROWS)
        held["out", i] = _dot(mix_s[rows, :], wout_ref[...])

    def m_res(i):
        rows = slice(i * OUT_ROWS, (i + 1) * OUT_ROWS)
        h_s[slot, rows, :] = x_ref[0, rows, :] + _rms(held.pop(("out", i)), gpost_ref[...])

    def f_norm():
        h2_s[...] = _rms(h_s[1 - slot], gpre2_ref[...]).astype(BF16)

    def f_up(c):
        w = jnp.concatenate([wup_ref[:, _ff_cols(gv, c)] for gv in range(2)], axis=1)
        held["up", c] = _dot(h2_s[...], w)

    def f_gate(c):
        conv = []
        up = held.pop(("up", c))
        for gv in range(2):
            u = up[:, gv * FF_CHUNK:(gv + 1) * FF_CHUNK]
            cols = _ff_cols(gv, c)
            prev8 = jnp.where(t_ffn > 0, carry_s[:, cols], 0.0)
            conv.append(cb_ref[:, cols] + cw_ref[0:1, cols] * _shift_rows(u, prev8, 2)
                        + cw_ref[1:2, cols] * _shift_rows(u, prev8, 1) + cw_ref[2:3, cols] * u)
            carry_s[:, cols] = u[tile - SUBLANES:, :]
        act_s[:, c * FF_CHUNK:(c + 1) * FF_CHUNK] = (_gelu_tanh(conv[0]) * conv[1]).astype(BF16)

    def f_down(n):
        held["down", n] = _dot(act_s[...], wdown_ref[:, n * DOWN_PIECE:(n + 1) * DOWN_PIECE])

    def f_res():
        f = jnp.concatenate([held.pop(("down", n)) for n in range(N_DOWN_PIECES)], axis=1)
        cst_ref[0] = carry_s[...]
        y_ref[0] = h_s[1 - slot] + _rms(f, gpost2_ref[...])

    assert n_blk == 4 and N_DOWN_PIECES == 4 and tile == 2 * OUT_ROWS and N_PROJ_PIECES <= N_FF_CHUNKS
    q0, q1, kv, u0, u1, vg0, vg1 = range(N_PROJ_PIECES)
    proj_order = [vg0, vg1, u0, u1, kv, q0, q1]
    slot_vector = {2: [m_vg], 3: [m_gmlp], 4: [m_u], 5: [m_kv], 7: [m_q]}

    def emit(mixer_on, ffn_on):
        def m(item, *a):
            if mixer_on:
                item(*a)

        def f(item, *a):
            if ffn_on:
                item(*a)

        f(f_norm)
        m(m_norm)
        f(f_up, 0)
        for c in range(N_FF_CHUNKS):
            if c + 1 < N_FF_CHUNKS:
                f(f_up, c + 1)
            if c < len(proj_order):
                m(m_proj, proj_order[c])
            f(f_gate, c)
            for item in slot_vector.get(c, []):
                m(item)
        f(f_down, 0)
        f(f_down, 1)
        m(m_block, 0)
        f(f_down, 2)
        m(m_block, 1)
        m(m_out, 0)
        f(f_down, 3)
        m(m_block, 2)
        m(m_block, 3)
        m(m_res, 0)
        m(m_out, 1)
        f(f_res)
        m(m_res, 1)

    pl.when(step == 0)(lambda: emit(True, False))
    pl.when((step > 0) & (step < n_total))(lambda: emit(True, True))
    pl.when(step == n_total)(lambda: emit(False, True))


def _prompt_layer(x, w_in, b_in, rel_bias, sinks, ln_g, ln_b, w_s, bs_full, g_att, g_gm, w_out,
                  g_pre, g_post, w_up, conv_w, conv_b, w_down, g_pre2, g_post2):
    B, L, _ = x.shape
    tile = PROMPT_TILE
    n_tiles = L // tile
    n_total = B * n_tiles
    bucket = jnp.asarray(_prompt_bucket_tile())

    def mix_tile(s):
        return jnp.minimum(s, n_total - 1)

    def ffn_tile(s):
        return jnp.maximum(s - 1, 0)

    kern = functools.partial(_prompt_kernel, tile=tile, n_tiles=n_tiles, n_total=n_total)
    return pl.pallas_call(
        kern,
        grid=(n_total + 1,),
        in_specs=[
            pl.BlockSpec((1, tile, D_MODEL), lambda s: (mix_tile(s) // n_tiles, mix_tile(s) % n_tiles, 0)),
            _const_spec((D_MODEL, IN_W)),
            _const_spec((1, IN_W)),
            _smem_spec(),
            _smem_spec(),
            _const_spec((BLOCK, 2 * BLOCK)),
            _const_spec((1, GMLP_W)),
            _const_spec((1, GMLP_W)),
            _const_spec((N_HEADS, BLOCK, BLOCK)),
            _const_spec((BLOCK, GMLP_W)),
            _const_spec((1, ATTN_W)),
            _const_spec((1, GMLP_W)),
            _const_spec((D_MODEL, D_MODEL)),
            _const_spec((1, D_MODEL)),
            _const_spec((1, D_MODEL)),
            _const_spec((D_MODEL, 2 * D_FF)),
            _const_spec((3, 2 * D_FF)),
            _const_spec((1, 2 * D_FF)),
            _const_spec((D_FF, D_MODEL)),
            _const_spec((1, D_MODEL)),
            _const_spec((1, D_MODEL)),
        ],
        out_specs=[
            pl.BlockSpec((1, tile, D_MODEL), lambda s: (ffn_tile(s) // n_tiles, ffn_tile(s) % n_tiles, 0)),
            pl.BlockSpec((1, BLOCK, KV_W), lambda s: (mix_tile(s) // n_tiles, 0, 0)),
            pl.BlockSpec((1, BLOCK, KV_W), lambda s: (mix_tile(s) // n_tiles, 0, 0)),
            pl.BlockSpec((1, SUBLANES, 2 * D_FF), lambda s: (ffn_tile(s) // n_tiles, 0, 0)),
        ],
        out_shape=[
            jax.ShapeDtypeStruct((B, L, D_MODEL), F32),
            jax.ShapeDtypeStruct((B, BLOCK, KV_W), F32),
            jax.ShapeDtypeStruct((B, BLOCK, KV_W), F32),
            jax.ShapeDtypeStruct((B, SUBLANES, 2 * D_FF), F32),
        ],
        scratch_shapes=[
            pltpu.VMEM((N_HEADS, BLOCK, 2 * BLOCK), F32),
            pltpu.VMEM((tile, D_MODEL), BF16),
            pltpu.VMEM((tile, IN_W), F32),
            pltpu.VMEM((tile, ATTN_W), BF16),
            pltpu.VMEM((8, tile + BLOCK, LANES), BF16),
            pltpu.VMEM((2, tile, GMLP_W), BF16),
            pltpu.VMEM((N_PAIRS, BLOCK, 2 * BLOCK), BF16),
            pltpu.VMEM((N_PAIRS, BLOCK, tile), F32),
            pltpu.VMEM((tile, D_MODEL), BF16),
            pltpu.VMEM((2, tile, D_MODEL), F32),
            pltpu.VMEM((tile, D_MODEL), BF16),
            pltpu.VMEM((tile, D_FF), BF16),
            pltpu.VMEM((SUBLANES, 2 * D_FF), F32),
        ],
        compiler_params=pltpu.CompilerParams(
            dimension_semantics=("arbitrary",),
            vmem_limit_bytes=VMEM_LIMIT_BYTES),
        name="prompt_layer",
    )(x, w_in, b_in, rel_bias, sinks, bucket, ln_g, ln_b, w_s, bs_full, g_att, g_gm, w_out,
      g_pre, g_post, w_up, conv_w, conv_b, w_down, g_pre2, g_post2)


def _sample_attn_kernel(x_ref, wqkv_ref, bqkv_ref, gpre_ref, ck_ref, cv_ref, relbt_ref, sink_ref,
                        bucket_ref, attn_ref, wk_ref, wv_ref,
                        bias_s, kext_s, vext_s, rep_s, *, group):
    i = pl.program_id(0)
    rows = group * SUBLANES

    @pl.when(i == 0)
    def _init():
        bt = jnp.broadcast_to(bucket_ref[...], (N_HEADS, 2 * BLOCK))
        acc = jnp.zeros((N_HEADS, 2 * BLOCK), F32)
        for bk in range(NUM_BUCKETS):
            acc = jnp.where(bt == bk, relbt_ref[:, bk:bk + 1], acc)
        bias_s[...] = acc
        kext_s[...] = jnp.zeros(kext_s.shape, F32)
        vext_s[...] = jnp.zeros(vext_s.shape, F32)

    xn = _rms(x_ref[...], gpre_ref[...])
    xr = jnp.concatenate(
        [jnp.broadcast_to(xn[s:s + 1, :], (SUBLANES, D_MODEL)) for s in range(group)], axis=0)
    proj = _dot(xr.astype(BF16), wqkv_ref[...]) + bqkv_ref[...]
    q = proj[:, :ATTN_W] * (HEAD_DIM ** -0.5)
    k3 = proj[:, ATTN_W:ATTN_W + KV_W].reshape(group, SUBLANES, LANES)
    v3 = proj[:, ATTN_W + KV_W:].reshape(group, SUBLANES, LANES)

    head = lax.broadcasted_iota(jnp.int32, (rows, LANES), 0) & (SUBLANES - 1)
    lane_half = lax.broadcasted_iota(jnp.int32, (rows, LANES), 1) // HEAD_DIM
    lane_lo = lane_half == 0
    qpad = jnp.zeros((rows, LANES), F32)
    for p in range(N_PAIRS):
        g = p // 2
        qp = q[:, p * LANES:(p + 1) * LANES]
        qr = pltpu.roll(qp, HEAD_DIM, axis=1)
        src_even = qp if g == 0 else qr
        src_odd = qr if g == 0 else qp
        qpad = jnp.where((head == 2 * p) & (lane_half == g), src_even, qpad)
        qpad = jnp.where((head == 2 * p + 1) & (lane_half == g), src_odd, qpad)
    qpad = qpad.reshape(group, SUBLANES, LANES).astype(BF16)

    ck = ck_ref[...]
    cv = cv_ref[...]
    kext_s[:, 0:BLOCK, :] = ck
    vext_s[:, 0:BLOCK, :] = cv
    kext_s[:, BLOCK:BLOCK + SUBLANES, :] = k3
    vext_s[:, BLOCK:BLOCK + SUBLANES, :] = v3

    s = jnp.einsum('shc,sjc->shj', qpad, kext_s[...].astype(BF16), preferred_element_type=F32)
    allowed = jnp.broadcast_to(bucket_ref[...], (N_HEADS, 2 * BLOCK)) >= 0
    s = jnp.where(allowed[None], s + bias_s[...][None], NEG_INF)
    pr, l = _softmax_with_sink(s, sink_ref[...][None])
    o = jnp.einsum('shj,sjc->shc', pr.astype(BF16), vext_s[...].astype(BF16),
                   preferred_element_type=F32)
    o = (o * (1.0 / l)).reshape(rows, LANES)

    om = jnp.where(lane_half == head // (N_HEADS // N_KV), o, 0.0)
    both = (om + pltpu.roll(om, HEAD_DIM, axis=1)).reshape(group, SUBLANES, LANES)
    lane_lo3 = lane_lo.reshape(group, SUBLANES, LANES)
    for p in range(N_PAIRS):
        even = jnp.broadcast_to(both[:, 2 * p:2 * p + 1, :], (group, SUBLANES, LANES))
        odd = jnp.broadcast_to(both[:, 2 * p + 1:2 * p + 2, :], (group, SUBLANES, LANES))
        rep_s[p] = jnp.where(lane_lo3, even, odd).reshape(rows, LANES)
    for p in range(N_PAIRS):
        attn_ref[:, p * LANES:(p + 1) * LANES] = rep_s[p, pl.ds(0, group, stride=SUBLANES), :]

    last = lax.broadcasted_iota(jnp.int32, (group, SUBLANES, LANES), 1) == SUBLANES - 1
    for cache, new3, out_ref in ((ck, k3, wk_ref), (cv, v3, wv_ref)):
        shifted = pltpu.roll(cache, BLOCK - 1, axis=1)
        out_ref[:, 0:BLOCK - SUBLANES, :] = shifted[:, 0:BLOCK - SUBLANES, :]
        out_ref[:, BLOCK - SUBLANES:, :] = jnp.where(last, new3, shifted[:, BLOCK - SUBLANES:, :])


def _sample_attention(x, wqkv, bqkv, g_pre, cache_k, cache_v, relb_t, sinks_col):
    S = x.shape[0]
    group = SAMPLE_GROUP
    bucket = jnp.asarray(_sample_bucket_row())
    kern = functools.partial(_sample_attn_kernel, group=group)
    return pl.pallas_call(
        kern,
        grid=(S // group,),
        in_specs=[
            pl.BlockSpec((group, D_MODEL), lambda i: (i, 0)),
            _const_spec((D_MODEL, QKV_W)),
            _const_spec((1, QKV_W)),
            _const_spec((1, D_MODEL)),
            pl.BlockSpec((group, BLOCK, KV_W), lambda i: (i, 0, 0)),
            pl.BlockSpec((group, BLOCK, KV_W), lambda i: (i, 0, 0)),
            _const_spec((N_HEADS, NUM_BUCKETS)),
            _const_spec((N_HEADS, 1)),
            _const_spec((1, 2 * BLOCK)),
        ],
        out_specs=[
            pl.BlockSpec((group, ATTN_W), lambda i: (i, 0)),
            pl.BlockSpec((group, BLOCK, KV_W), lambda i: (i, 0, 0)),
            pl.BlockSpec((group, BLOCK, KV_W), lambda i: (i, 0, 0)),
        ],
        out_shape=[
            jax.ShapeDtypeStruct((S, ATTN_W), F32),
            jax.ShapeDtypeStruct((S, BLOCK, KV_W), F32),
            jax.ShapeDtypeStruct((S, BLOCK, KV_W), F32),
        ],
        scratch_shapes=[
            pltpu.VMEM((N_HEADS, 2 * BLOCK), F32),
            pltpu.VMEM((group, 2 * BLOCK, LANES), F32),
            pltpu.VMEM((group, 2 * BLOCK, LANES), F32),
            pltpu.VMEM((N_PAIRS, group * SUBLANES, LANES), F32),
        ],
        compiler_params=pltpu.CompilerParams(
            dimension_semantics=("arbitrary",),
            vmem_limit_bytes=VMEM_LIMIT_BYTES),
        name="sample_attention",
    )(x, wqkv, bqkv, g_pre, cache_k, cache_v, relb_t, sinks_col, bucket)


def _sample_tail_kernel(x_ref, attn_ref, st_ref, win_ref, bin_ref, lng_ref, lnb_ref, ws0_ref, bs0_ref,
                        gatt_ref, ggm_ref, wout_ref, gpre_ref, gpost_ref, gpre2_ref, gpost2_ref,
                        wup_ref, cw_ref, cb_ref, wdown_ref,
                        y_ref, vn_ref, nst_ref):
    x = x_ref[...]
    xn = _rms(x, gpre_ref[...])
    puv = _dot(xn.astype(BF16), win_ref[:, QKV_W:]) + bin_ref[:, QKV_W:]
    u = _gelu_erf(puv[:, :GMLP_W])
    vn = _layer_norm(_gelu_erf(puv[:, GMLP_W:]), lng_ref[...], lnb_ref[...])
    vn_ref[...] = vn
    gm = u * (ws0_ref[...] * vn + bs0_ref[...])
    mix = jnp.concatenate([_rms(attn_ref[...], gatt_ref[...]), _rms(gm, ggm_ref[...])], axis=1)
    h = x + _rms(_dot(mix.astype(BF16), wout_ref[...]), gpost_ref[...])

    h2 = _rms(h, gpre2_ref[...]).astype(BF16)
    two_ff = 2 * D_FF
    nst_ref[:, 0:two_ff] = st_ref[:, two_ff:]
    acc = jnp.zeros((x.shape[0], D_MODEL), F32)
    for c in range(N_FF_CHUNKS):
        conv = []
        for gv in range(2):
            cols = _ff_cols(gv, c)
            cols1 = slice(two_ff + cols.start, two_ff + cols.stop)
            u_c = _dot(h2, wup_ref[:, cols])
            conv.append(cb_ref[:, cols] + cw_ref[0:1, cols] * st_ref[:, cols]
                        + cw_ref[1:2, cols] * st_ref[:, cols1] + cw_ref[2:3, cols] * u_c)
            nst_ref[:, cols1] = u_c
        act = (_gelu_tanh(conv[0]) * conv[1]).astype(BF16)
        acc = acc + _dot(act, wdown_ref[c * FF_CHUNK:(c + 1) * FF_CHUNK, :])
    y_ref[...] = h + _rms(acc, gpost2_ref[...])


def _sample_tail(x, attn, state, w_in, b_in, ln_g, ln_b, ws0, bs0, g_att, g_gm, w_out, g_pre, g_post,
                 g_pre2, g_post2, w_up, conv_w, conv_b, w_down):
    S = x.shape[0]
    args = (x, attn, state, w_in, b_in, ln_g, ln_b, ws0, bs0, g_att, g_gm, w_out, g_pre, g_post,
            g_pre2, g_post2, w_up, conv_w, conv_b, w_down)
    return pl.pallas_call(
        _sample_tail_kernel,
        grid=(1,),
        in_specs=[_const_spec(a.shape) for a in args],
        out_specs=[
            pl.BlockSpec((S, D_MODEL), lambda i: (0, 0)),
            pl.BlockSpec((S, GMLP_W), lambda i: (0, 0)),
            pl.BlockSpec((S, 4 * D_FF), lambda i: (0, 0)),
        ],
        out_shape=[
            jax.ShapeDtypeStruct((S, D_MODEL), F32),
            jax.ShapeDtypeStruct((S, GMLP_W), F32),
            jax.ShapeDtypeStruct((S, 4 * D_FF), F32),
        ],
        compiler_params=pltpu.CompilerParams(
            dimension_semantics=("arbitrary",),
            vmem_limit_bytes=VMEM_LIMIT_BYTES),
        name="sample_tail",
    )(*args)


def kernel(x_prompt, x_sample, cache_win_k, cache_win_v, state_ffn_conv, rel_bias, w_in, b_in, attn_sinks, gmlp_ln_g, gmlp_ln_b, gmlp_w_s, gmlp_b_s, g_attn_out, g_gmlp_out, w_out, g_pre_mix, g_post_mix, g_pre_ffn, g_post_ffn, w_up, ffn_conv_w, ffn_conv_b, w_down):
    depth = w_in.shape[0]
    assert depth == 1, "single-layer step"
    B, L, _ = x_prompt.shape
    S = x_sample.shape[0]
    assert x_sample.shape[1] == 1 and L % PROMPT_TILE == 0 and S % SAMPLE_GROUP == 0
    assert cache_win_k.shape[2] == BLOCK

    row = lambda a: a.reshape(1, -1)
    w_in_b = w_in[0].astype(BF16)
    w_out_b = w_out[0].astype(BF16)
    b_in_r = row(b_in[0])
    w_up_b = w_up[0].astype(BF16)
    w_down_b = w_down[0].astype(BF16)
    conv_w, conv_b = ffn_conv_w[0], row(ffn_conv_b[0])
    bs_full = jnp.repeat(gmlp_b_s[0].T, HEAD_DIM, axis=1)
    ln_g, ln_b = row(gmlp_ln_g[0]), row(gmlp_ln_b[0])
    g_att, g_gm = row(g_attn_out[0]), row(g_gmlp_out[0])
    g_pre, g_post = row(g_pre_mix[0]), row(g_post_mix[0])
    g_pre2, g_post2 = row(g_pre_ffn[0]), row(g_post_ffn[0])

    y_p, wk_p, wv_p, cst_p = _prompt_layer(
        x_prompt, w_in_b, b_in_r, rel_bias, attn_sinks[0], ln_g, ln_b, gmlp_w_s[0], bs_full,
        g_att, g_gm, w_out_b, g_pre, g_post, w_up_b, conv_w, conv_b, w_down_b, g_pre2, g_post2)
    conv_p = cst_p[:, SUBLANES - 2:, :].reshape(1, B, 2, 2 * D_FF)

    xs = x_sample.reshape(S, D_MODEL)
    attn_s, wk_s, wv_s = _sample_attention(
        xs, w_in_b, b_in_r, g_pre,
        cache_win_k[0].reshape(S, BLOCK, KV_W), cache_win_v[0].reshape(S, BLOCK, KV_W),
        rel_bias.T, attn_sinks[0].reshape(N_HEADS, 1))
    ws0 = row(jnp.repeat(gmlp_w_s[0][:, 0, 0], HEAD_DIM))
    bs0 = row(jnp.repeat(gmlp_b_s[0][:, 0], HEAD_DIM))
    y_s, vn_s, nst_s = _sample_tail(
        xs, attn_s, state_ffn_conv[0].reshape(S, 4 * D_FF), w_in_b, b_in_r,
        ln_g, ln_b, ws0, bs0, g_att, g_gm, w_out_b, g_pre, g_post, g_pre2, g_post2,
        w_up_b, conv_w, conv_b, w_down_b)

    kv_shape_p = (1, B, BLOCK, N_KV, HEAD_DIM)
    kv_shape_s = (1, S, BLOCK, N_KV, HEAD_DIM)
    return (y_p,
            y_s.reshape(S, 1, D_MODEL),
            wk_p.reshape(kv_shape_p), wv_p.reshape(kv_shape_p),
            wk_s.reshape(kv_shape_s), wv_s.reshape(kv_shape_s),
            vn_s.reshape(1, S, 1, GMLP_W),
            conv_p,
            nst_s.reshape(1, S, 2, 2 * D_FF))
```

```python
import functools
import math

import numpy as np
import jax
import jax.numpy as jnp
from jax import lax
from jax.experimental import pallas as pl
from jax.experimental.pallas import tpu as pltpu

D_MODEL = 1024
HEAD_DIM = 64
N_HEADS = 8
N_KV = 2
N_PAIRS = N_HEADS // 2
BLOCK = 128
ATTN_W = N_HEADS * HEAD_DIM
KV_W = N_KV * HEAD_DIM
GMLP_W = 512
IN_W = ATTN_W + 2 * KV_W + 2 * GMLP_W
QKV_W = ATTN_W + 2 * KV_W
D_FF = 2816
FF_CHUNK = 256
N_FF_CHUNKS = D_FF // FF_CHUNK
NUM_BUCKETS = 32
MAX_DISTANCE = 128
EPS = 1e-6
NEG_INF = -1e30
SQRT_HALF = np.sqrt(0.5).astype(np.float32)
SQRT_2_OVER_PI = np.sqrt(2 / np.pi).astype(np.float32)

LANES = 128
SUBLANES = 8
VMEM_LIMIT_BYTES = 58 * 1024 * 1024

PROMPT_TILE = 512
SAMPLE_GROUP = 16

F32 = jnp.float32
BF16 = jnp.bfloat16


def _t5_bucket_np(dist):
    n = np.maximum(dist, 0)
    max_exact = NUM_BUCKETS // 2
    nf = np.maximum(n, 1).astype(np.float32)
    large = max_exact + (np.log(nf / max_exact) / math.log(MAX_DISTANCE / max_exact)
                         * (NUM_BUCKETS - max_exact)).astype(np.int32)
    large = np.minimum(large, NUM_BUCKETS - 1)
    return np.where(n < max_exact, n, large).astype(np.int32)


def _prompt_bucket_tile():
    r = np.arange(BLOCK)[:, None]
    c = np.arange(2 * BLOCK)[None, :]
    dist = r + BLOCK - c
    ok = (dist >= 0) & (dist <= BLOCK)
    return np.where(ok, _t5_bucket_np(dist), -1).astype(np.int32)


def _sample_bucket_row():
    c = np.arange(2 * BLOCK)[None, :]
    dist = BLOCK - c
    ok = dist >= 0
    return np.where(ok, _t5_bucket_np(dist), -1).astype(np.int32)


def _rms(x, g):
    ms = jnp.mean(x * x, axis=-1, keepdims=True)
    return x * lax.rsqrt(ms + EPS) * g


def _layer_norm(x, g, b):
    mu = jnp.mean(x, axis=-1, keepdims=True)
    xc = x - mu
    y = xc * lax.rsqrt(jnp.mean(xc * xc, axis=-1, keepdims=True) + EPS)
    return y * g + b


def _gelu_erf(x):
    return 0.5 * x * (1.0 + lax.erf(x * SQRT_HALF))


def _gelu_tanh(x):
    cdf = 0.5 * (1.0 + jnp.tanh(SQRT_2_OVER_PI * (x + 0.044715 * (x * x * x))))
    return x * cdf


def _softmax_with_sink(s, sink):
    m = jnp.maximum(jnp.max(s, axis=-1, keepdims=True), sink)
    p = jnp.exp(s - m)
    l = jnp.sum(p, axis=-1, keepdims=True) + jnp.exp(sink - m)
    return p, l


def _dot(a, b):
    return jnp.dot(a, b, preferred_element_type=F32)


def _dot_nt(a, b):
    return lax.dot_general(a, b, (((1,), (1,)), ((), ())), preferred_element_type=F32)


def _shift_rows(u, prev8, n):
    s = pltpu.roll(u, n, axis=0)
    top = jnp.where(lax.broadcasted_iota(jnp.int32, prev8.shape, 0) < n,
                    pltpu.roll(prev8, n, axis=0), s[0:SUBLANES])
    return jnp.concatenate([top, s[SUBLANES:]], axis=0)


def _ff_cols(gv, c):
    start = gv * D_FF + c * FF_CHUNK
    return slice(start, start + FF_CHUNK)


def _const_spec(shape):
    nd = len(shape)
    return pl.BlockSpec(shape, lambda *_: (0,) * nd, pipeline_mode=pl.Buffered(1))


def _smem_spec():
    return pl.BlockSpec(memory_space=pltpu.SMEM)


PROJ_PIECE = 256
N_PROJ_PIECES = IN_W // PROJ_PIECE
DOWN_PIECE = 256
N_DOWN_PIECES = D_MODEL // DOWN_PIECE
OUT_ROWS = 256


def _prompt_kernel(x_ref, win_ref, bin_ref, relb_ref, sink_ref, bucket_ref, lng_ref, lnb_ref,
                   ws_ref, bs_ref, gatt_ref, ggm_ref, wout_ref, gpre_ref, gpost_ref,
                   wup_ref, cw_ref, cb_ref, wdown_ref, gpre2_ref, gpost2_ref,
                   y_ref, wk_ref, wv_ref, cst_ref,
                   bias_s, xb_s, proj_s, q_s, kv_s, vnp_s, wsp_s, gmix_s, mix_s, h_s, h2_s, act_s, carry_s,
                   *, tile, n_tiles, n_total):
    step = pl.program_id(0)
    n_blk = tile // BLOCK
    t_mix = jnp.minimum(step, n_total - 1) % n_tiles
    t_ffn = jnp.maximum(step - 1, 0) % n_tiles
    slot = step % 2

    @pl.when(step == 0)
    def _init():
        bt = bucket_ref[...]
        for h in range(N_HEADS):
            acc = jnp.zeros((BLOCK, 2 * BLOCK), F32)
            for bk in range(NUM_BUCKETS):
                acc = jnp.where(bt == bk, relb_ref[bk, h], acc)
            bias_s[h] = acc
        row = lax.broadcasted_iota(jnp.int32, (BLOCK, BLOCK), 0)
        col = lax.broadcasted_iota(jnp.int32, (BLOCK, BLOCK), 1)
        tril = (row >= col).astype(F32)
        for p in range(N_PAIRS):
            wsp_s[p] = jnp.concatenate(
                [ws_ref[2 * p] * tril, ws_ref[2 * p + 1] * tril], axis=1).astype(BF16)
        kv_s[...] = jnp.zeros(kv_s.shape, BF16)
        carry_s[...] = jnp.zeros(carry_s.shape, F32)
        h_s[...] = jnp.zeros(h_s.shape, F32)

    held = {}

    def m_norm():
        prev = kv_s[:, tile:tile + BLOCK, :]
        kv_s[:, 0:BLOCK, :] = jnp.where(t_mix > 0, prev, jnp.zeros_like(prev))
        xb_s[...] = _rms(x_ref[0], gpre_ref[...]).astype(BF16)

    def m_proj(i):
        cols = slice(i * PROJ_PIECE, (i + 1) * PROJ_PIECE)
        proj_s[:, cols] = _dot(xb_s[...], win_ref[:, cols]) + bin_ref[:, cols]

    def m_q():
        q_s[...] = (proj_s[:, :ATTN_W] * (HEAD_DIM ** -0.5)).astype(BF16)

    def m_kv():
        k = proj_s[:, ATTN_W:ATTN_W + KV_W]
        v = proj_s[:, ATTN_W + KV_W:QKV_W]
        wk_ref[0] = k[tile - BLOCK:, :]
        wv_ref[0] = v[tile - BLOCK:, :]
        lo = lax.broadcasted_iota(jnp.int32, (tile, LANES), 1) < HEAD_DIM
        for base, val in ((0, k), (4, v)):
            rolled = pltpu.roll(val, HEAD_DIM, axis=1)
            kv_s[base + 0, BLOCK:, :] = jnp.where(lo, val, 0.0).astype(BF16)
            kv_s[base + 1, BLOCK:, :] = jnp.where(lo, 0.0, rolled).astype(BF16)
            kv_s[base + 2, BLOCK:, :] = jnp.where(lo, rolled, 0.0).astype(BF16)
            kv_s[base + 3, BLOCK:, :] = jnp.where(lo, 0.0, val).astype(BF16)

    def m_u():
        ucols = slice(QKV_W, QKV_W + GMLP_W)
        proj_s[:, ucols] = _gelu_erf(proj_s[:, ucols])

    def m_vg():
        vn = _layer_norm(_gelu_erf(proj_s[:, QKV_W + GMLP_W:]), lng_ref[...], lnb_ref[...])
        lo4 = (lax.broadcasted_iota(jnp.int32, (tile, GMLP_W), 1) & (LANES - 1)) < HEAD_DIM
        vnp_s[0] = jnp.where(lo4, vn, 0.0).astype(BF16)
        vnp_s[1] = jnp.where(lo4, 0.0, vn).astype(BF16)

    def m_gmlp():
        for p in range(N_PAIRS):
            cols = slice(p * LANES, (p + 1) * LANES)
            rhs = jnp.concatenate(
                [jnp.concatenate([vnp_s[0, j * BLOCK:(j + 1) * BLOCK, cols],
                                  vnp_s[1, j * BLOCK:(j + 1) * BLOCK, cols]], axis=0)
                 for j in range(n_blk)], axis=1)
            gmix_s[p] = _dot(wsp_s[p], rhs)

    def m_block(j):
        r = lax.broadcasted_iota(jnp.int32, (BLOCK, 2 * BLOCK), 0)
        c = lax.broadcasted_iota(jnp.int32, (BLOCK, 2 * BLOCK), 1)
        dist = r + BLOCK - c
        allowed = (dist >= 0) & (dist <= BLOCK)
        if j == 0:
            allowed = allowed & (c >= jnp.where(t_mix > 0, 0, BLOCK))
        lane_lo = lax.broadcasted_iota(jnp.int32, (BLOCK, LANES), 1) < HEAD_DIM
        rows = slice(j * BLOCK, (j + 1) * BLOCK)
        krows = slice(j * BLOCK, (j + 2) * BLOCK)

        attn_parts = []
        for g in range(N_KV):
            pairs = (2 * g, 2 * g + 1)
            qg = jnp.concatenate([q_s[rows, p * LANES:(p + 1) * LANES] for p in pairs], axis=0)
            kcat = jnp.concatenate([kv_s[2 * g, krows, :], kv_s[2 * g + 1, krows, :]], axis=0)
            s = _dot_nt(qg, kcat)
            probs, denoms = [], []
            for i, p in enumerate(pairs):
                pair_probs = []
                for par in range(2):
                    h = 2 * p + par
                    sh = s[i * BLOCK:(i + 1) * BLOCK, par * 2 * BLOCK:(par + 1) * 2 * BLOCK]
                    sh = jnp.where(allowed, sh + bias_s[h], NEG_INF)
                    ph, lh = _softmax_with_sink(sh, sink_ref[h])
                    pair_probs.append(ph.astype(BF16))
                    denoms.append(lh)
                probs.append(jnp.concatenate(pair_probs, axis=1))
            vcat = jnp.concatenate([kv_s[4 + 2 * g, krows, :], kv_s[4 + 2 * g + 1, krows, :]], axis=0)
            o = _dot(jnp.concatenate(probs, axis=0), vcat)
            for i in range(2):
                inv = jnp.where(lane_lo, 1.0 / denoms[2 * i], 1.0 / denoms[2 * i + 1])
                attn_parts.append(o[i * BLOCK:(i + 1) * BLOCK] * inv)

        gm_parts = []
        for p in range(N_PAIRS):
            cols = slice(p * LANES, (p + 1) * LANES)
            mixed = gmix_s[p, :, rows] + bs_ref[:, cols]
            ucols = slice(QKV_W + p * LANES, QKV_W + (p + 1) * LANES)
            gm_parts.append(proj_s[rows, ucols] * mixed)

        attn = jnp.concatenate(attn_parts, axis=1)
        gm = jnp.concatenate(gm_parts, axis=1)
        mix_s[rows, 0:ATTN_W] = _rms(attn, gatt_ref[...]).astype(BF16)
        mix_s[rows, ATTN_W:] = _rms(gm, ggm_ref[...]).astype(BF16)

    def m_out(i):
        rows = slice(i * OUT_ROWS, (i + 1) * OUT_ROWS)
        held["out", i] = _dot(mix_s[rows, :], wout_ref[...])

    def m_res(i):
        rows = slice(i * OUT_ROWS, (i + 1) * OUT_ROWS)
        h_s[slot, rows, :] = x_ref[0, rows, :] + _rms(held.pop(("out", i)), gpost_ref[...])

    def f_norm():
        h2_s[...] = _rms(h_s[1 - slot], gpre2_ref[...]).astype(BF16)

    def f_up(c):
        w = jnp.concatenate([wup_ref[:, _ff_cols(gv, c)] for gv in range(2)], axis=1)
        held["up", c] = _dot(h2_s[...], w)

    def f_gate(c):
        conv = []
        up = held.pop(("up", c))
        for gv in range(2):
            u = up[:, gv * FF_CHUNK:(gv + 1) * FF_CHUNK]
            cols = _ff_cols(gv, c)
            prev8 = jnp.where(t_ffn > 0, carry_s[:, cols], 0.0)
            conv.append(cb_ref[:, cols] + cw_ref[0:1, cols] * _shift_rows(u, prev8, 2)
                        + cw_ref[1:2, cols] * _shift_rows(u, prev8, 1) + cw_ref[2:3, cols] * u)
            carry_s[:, cols] = u[tile - SUBLANES:, :]
        act_s[:, c * FF_CHUNK:(c + 1) * FF_CHUNK] = (_gelu_tanh(conv[0]) * conv[1]).astype(BF16)

    def f_down(n):
        held["down", n] = _dot(act_s[...], wdown_ref[:, n * DOWN_PIECE:(n + 1) * DOWN_PIECE])

    def f_res():
        f = jnp.concatenate([held.pop(("down", n)) for n in range(N_DOWN_PIECES)], axis=1)
        cst_ref[0] = carry_s[...]
        y_ref[0] = h_s[1 - slot] + _rms(f, gpost2_ref[...])

    assert n_blk == 4 and N_DOWN_PIECES == 4 and tile == 2 * OUT_ROWS and N_PROJ_PIECES <= N_FF_CHUNKS
    q0, q1, kv, u0, u1, vg0, vg1 = range(N_PROJ_PIECES)
    proj_order = [vg0, vg1, u0, u1, kv, q0, q1]
    slot_vector = {2: [m_vg], 3: [m_gmlp], 4: [m_u], 5: [m_kv], 7: [m_q]}

    def emit(mixer_on, ffn_on):
        def m(item, *a):
            if mixer_on:
                item(*a)

        def f(item, *a):
            if ffn_on:
                item(*a)

        f(f_norm)
        m(m_norm)
        f(f_up, 0)
        for c in range(N_FF_CHUNKS):
            if c + 1 < N_FF_CHUNKS:
                f(f_up, c + 1)
            if c < len(proj_order):
                m(m_proj, proj_order[c])
            f(f_gate, c)
            for item in slot_vector.get(c, []):
                m(item)
        f(f_down, 0)
        f(f_down, 1)
        m(m_block, 0)
        f(f_down, 2)
        m(m_block, 1)
        m(m_out, 0)
        f(f_down, 3)
        m(m_block, 2)
        m(m_block, 3)
        m(m_res, 0)
        m(m_out, 1)
        f(f_res)
        m(m_res, 1)

    emit(True, True)


def _prompt_layer(x, w_in, b_in, rel_bias, sinks, ln_g, ln_b, w_s, bs_full, g_att, g_gm, w_out,
                  g_pre, g_post, w_up, conv_w, conv_b, w_down, g_pre2, g_post2):
    B, L, _ = x.shape
    tile = PROMPT_TILE
    n_tiles = L // tile
    n_total = B * n_tiles
    bucket = jnp.asarray(_prompt_bucket_tile())

    def mix_tile(s):
        return jnp.minimum(s, n_total - 1)

    def ffn_tile(s):
        return jnp.maximum(s - 1, 0)

    kern = functools.partial(_prompt_kernel, tile=tile, n_tiles=n_tiles, n_total=n_total)
    return pl.pallas_call(
        kern,
        grid=(n_total + 1,),
        in_specs=[
            pl.BlockSpec((1, tile, D_MODEL), lambda s: (mix_tile(s) // n_tiles, mix_tile(s) % n_tiles, 0)),
            _const_spec((D_MODEL, IN_W)),
            _const_spec((1, IN_W)),
            _smem_spec(),
            _smem_spec(),
            _const_spec((BLOCK, 2 * BLOCK)),
            _const_spec((1, GMLP_W)),
            _const_spec((1, GMLP_W)),
            _const_spec((N_HEADS, BLOCK, BLOCK)),
            _const_spec((BLOCK, GMLP_W)),
            _const_spec((1, ATTN_W)),
            _const_spec((1, GMLP_W)),
            _const_spec((D_MODEL, D_MODEL)),
            _const_spec((1, D_MODEL)),
            _const_spec((1, D_MODEL)),
            _const_spec((D_MODEL, 2 * D_FF)),
            _const_spec((3, 2 * D_FF)),
            _const_spec((1, 2 * D_FF)),
            _const_spec((D_FF, D_MODEL)),
            _const_spec((1, D_MODEL)),
            _const_spec((1, D_MODEL)),
        ],
        out_specs=[
            pl.BlockSpec((1, tile, D_MODEL), lambda s: (ffn_tile(s) // n_tiles, ffn_tile(s) % n_tiles, 0)),
            pl.BlockSpec((1, BLOCK, KV_W), lambda s: (mix_tile(s) // n_tiles, 0, 0)),
            pl.BlockSpec((1, BLOCK, KV_W), lambda s: (mix_tile(s) // n_tiles, 0, 0)),
            pl.BlockSpec((1, SUBLANES, 2 * D_FF), lambda s: (ffn_tile(s) // n_tiles, 0, 0)),
        ],
        out_shape=[
            jax.ShapeDtypeStruct((B, L, D_MODEL), F32),
            jax.ShapeDtypeStruct((B, BLOCK, KV_W), F32),
            jax.ShapeDtypeStruct((B, BLOCK, KV_W), F32),
            jax.ShapeDtypeStruct((B, SUBLANES, 2 * D_FF), F32),
        ],
        scratch_shapes=[
            pltpu.VMEM((N_HEADS, BLOCK, 2 * BLOCK), F32),
            pltpu.VMEM((tile, D_MODEL), BF16),
            pltpu.VMEM((tile, IN_W), F32),
            pltpu.VMEM((tile, ATTN_W), BF16),
            pltpu.VMEM((8, tile + BLOCK, LANES), BF16),
            pltpu.VMEM((2, tile, GMLP_W), BF16),
            pltpu.VMEM((N_PAIRS, BLOCK, 2 * BLOCK), BF16),
            pltpu.VMEM((N_PAIRS, BLOCK, tile), F32),
            pltpu.VMEM((tile, D_MODEL), BF16),
            pltpu.VMEM((2, tile, D_MODEL), F32),
            pltpu.VMEM((tile, D_MODEL), BF16),
            pltpu.VMEM((tile, D_FF), BF16),
            pltpu.VMEM((SUBLANES, 2 * D_FF), F32),
        ],
        compiler_params=pltpu.CompilerParams(
            dimension_semantics=("arbitrary",),
            vmem_limit_bytes=VMEM_LIMIT_BYTES),
        name="prompt_layer",
    )(x, w_in, b_in, rel_bias, sinks, bucket, ln_g, ln_b, w_s, bs_full, g_att, g_gm, w_out,
      g_pre, g_post, w_up, conv_w, conv_b, w_down, g_pre2, g_post2)


def _sample_attn_kernel(x_ref, wqkv_ref, bqkv_ref, gpre_ref, ck_ref, cv_ref, relbt_ref, sink_ref,
                        bucket_ref, attn_ref, wk_ref, wv_ref,
                        bias_s, kext_s, vext_s, rep_s, *, group):
    i = pl.program_id(0)
    rows = group * SUBLANES

    @pl.when(i == 0)
    def _init():
        bt = jnp.broadcast_to(bucket_ref[...], (N_HEADS, 2 * BLOCK))
        acc = jnp.zeros((N_HEADS, 2 * BLOCK), F32)
        for bk in range(NUM_BUCKETS):
            acc = jnp.where(bt == bk, relbt_ref[:, bk:bk + 1], acc)
        bias_s[...] = acc
        kext_s[...] = jnp.zeros(kext_s.shape, F32)
        vext_s[...] = jnp.zeros(vext_s.shape, F32)

    xn = _rms(x_ref[...], gpre_ref[...])
    xr = jnp.concatenate(
        [jnp.broadcast_to(xn[s:s + 1, :], (SUBLANES, D_MODEL)) for s in range(group)], axis=0)
    proj = _dot(xr.astype(BF16), wqkv_ref[...]) + bqkv_ref[...]
    q = proj[:, :ATTN_W] * (HEAD_DIM ** -0.5)
    k3 = proj[:, ATTN_W:ATTN_W + KV_W].reshape(group, SUBLANES, LANES)
    v3 = proj[:, ATTN_W + KV_W:].reshape(group, SUBLANES, LANES)

    head = lax.broadcasted_iota(jnp.int32, (rows, LANES), 0) & (SUBLANES - 1)
    lane_half = lax.broadcasted_iota(jnp.int32, (rows, LANES), 1) // HEAD_DIM
    lane_lo = lane_half == 0
    qpad = jnp.zeros((rows, LANES), F32)
    for p in range(N_PAIRS):
        g = p // 2
        qp = q[:, p * LANES:(p + 1) * LANES]
        qr = pltpu.roll(qp, HEAD_DIM, axis=1)
        src_even = qp if g == 0 else qr
        src_odd = qr if g == 0 else qp
        qpad = jnp.where((head == 2 * p) & (lane_half == g), src_even, qpad)
        qpad = jnp.where((head == 2 * p + 1) & (lane_half == g), src_odd, qpad)
    qpad = qpad.reshape(group, SUBLANES, LANES).astype(BF16)

    ck = ck_ref[...]
    cv = cv_ref[...]
    kext_s[:, 0:BLOCK, :] = ck
    vext_s[:, 0:BLOCK, :] = cv
    kext_s[:, BLOCK:BLOCK + SUBLANES, :] = k3
    vext_s[:, BLOCK:BLOCK + SUBLANES, :] = v3

    s = jnp.einsum('shc,sjc->shj', qpad, kext_s[...].astype(BF16), preferred_element_type=F32)
    allowed = jnp.broadcast_to(bucket_ref[...], (N_HEADS, 2 * BLOCK)) >= 0
    s = jnp.where(allowed[None], s + bias_s[...][None], NEG_INF)
    pr, l = _softmax_with_sink(s, sink_ref[...][None])
    o = jnp.einsum('shj,sjc->shc', pr.astype(BF16), vext_s[...].astype(BF16),
                   preferred_element_type=F32)
    o = (o * (1.0 / l)).reshape(rows, LANES)

    om = jnp.where(lane_half == head // (N_HEADS // N_KV), o, 0.0)
    both = (om + pltpu.roll(om, HEAD_DIM, axis=1)).reshape(group, SUBLANES, LANES)
    lane_lo3 = lane_lo.reshape(group, SUBLANES, LANES)
    for p in range(N_PAIRS):
        even = jnp.broadcast_to(both[:, 2 * p:2 * p + 1, :], (group, SUBLANES, LANES))
        odd = jnp.broadcast_to(both[:, 2 * p + 1:2 * p + 2, :], (group, SUBLANES, LANES))
        rep_s[p] = jnp.where(lane_lo3, even, odd).reshape(rows, LANES)
    for p in range(N_PAIRS):
        attn_ref[:, p * LANES:(p + 1) * LANES] = rep_s[p, pl.ds(0, group, stride=SUBLANES), :]

    last = lax.broadcasted_iota(jnp.int32, (group, SUBLANES, LANES), 1) == SUBLANES - 1
    for cache, new3, out_ref in ((ck, k3, wk_ref), (cv, v3, wv_ref)):
        shifted = pltpu.roll(cache, BLOCK - 1, axis=1)
        out_ref[:, 0:BLOCK - SUBLANES, :] = shifted[:, 0:BLOCK - SUBLANES, :]
        out_ref[:, BLOCK - SUBLANES:, :] = jnp.where(last, new3, shifted[:, BLOCK - SUBLANES:, :])


def _sample_attention(x, wqkv, bqkv, g_pre, cache_k, cache_v, relb_t, sinks_col):
    S = x.shape[0]
    group = SAMPLE_GROUP
    bucket = jnp.asarray(_sample_bucket_row())
    kern = functools.partial(_sample_attn_kernel, group=group)
    return pl.pallas_call(
        kern,
        grid=(S // group,),
        in_specs=[
            pl.BlockSpec((group, D_MODEL), lambda i: (i, 0)),
            _const_spec((D_MODEL, QKV_W)),
            _const_spec((1, QKV_W)),
            _const_spec((1, D_MODEL)),
            pl.BlockSpec((group, BLOCK, KV_W), lambda i: (i, 0, 0)),
            pl.BlockSpec((group, BLOCK, KV_W), lambda i: (i, 0, 0)),
            _const_spec((N_HEADS, NUM_BUCKETS)),
            _const_spec((N_HEADS, 1)),
            _const_spec((1, 2 * BLOCK)),
        ],
        out_specs=[
            pl.BlockSpec((group, ATTN_W), lambda i: (i, 0)),
            pl.BlockSpec((group, BLOCK, KV_W), lambda i: (i, 0, 0)),
            pl.BlockSpec((group, BLOCK, KV_W), lambda i: (i, 0, 0)),
        ],
        out_shape=[
            jax.ShapeDtypeStruct((S, ATTN_W), F32),
            jax.ShapeDtypeStruct((S, BLOCK, KV_W), F32),
            jax.ShapeDtypeStruct((S, BLOCK, KV_W), F32),
        ],
        scratch_shapes=[
            pltpu.VMEM((N_HEADS, 2 * BLOCK), F32),
            pltpu.VMEM((group, 2 * BLOCK, LANES), F32),
            pltpu.VMEM((group, 2 * BLOCK, LANES), F32),
            pltpu.VMEM((N_PAIRS, group * SUBLANES, LANES), F32),
        ],
        compiler_params=pltpu.CompilerParams(
            dimension_semantics=("arbitrary",),
            vmem_limit_bytes=VMEM_LIMIT_BYTES),
        name="sample_attention",
    )(x, wqkv, bqkv, g_pre, cache_k, cache_v, relb_t, sinks_col, bucket)


def _sample_tail_kernel(x_ref, attn_ref, st_ref, win_ref, bin_ref, lng_ref, lnb_ref, ws0_ref, bs0_ref,
                        gatt_ref, ggm_ref, wout_ref, gpre_ref, gpost_ref, gpre2_ref, gpost2_ref,
                        wup_ref, cw_ref, cb_ref, wdown_ref,
                        y_ref, vn_ref, nst_ref):
    x = x_ref[...]
    xn = _rms(x, gpre_ref[...])
    puv = _dot(xn.astype(BF16), win_ref[:, QKV_W:]) + bin_ref[:, QKV_W:]
    u = _gelu_erf(puv[:, :GMLP_W])
    vn = _layer_norm(_gelu_erf(puv[:, GMLP_W:]), lng_ref[...], lnb_ref[...])
    vn_ref[...] = vn
    gm = u * (ws0_ref[...] * vn + bs0_ref[...])
    mix = jnp.concatenate([_rms(attn_ref[...], gatt_ref[...]), _rms(gm, ggm_ref[...])], axis=1)
    h = x + _rms(_dot(mix.astype(BF16), wout_ref[...]), gpost_ref[...])

    h2 = _rms(h, gpre2_ref[...]).astype(BF16)
    two_ff = 2 * D_FF
    nst_ref[:, 0:two_ff] = st_ref[:, two_ff:]
    acc = jnp.zeros((x.shape[0], D_MODEL), F32)
    for c in range(N_FF_CHUNKS):
        conv = []
        for gv in range(2):
            cols = _ff_cols(gv, c)
            cols1 = slice(two_ff + cols.start, two_ff + cols.stop)
            u_c = _dot(h2, wup_ref[:, cols])
            conv.append(cb_ref[:, cols] + cw_ref[0:1, cols] * st_ref[:, cols]
                        + cw_ref[1:2, cols] * st_ref[:, cols1] + cw_ref[2:3, cols] * u_c)
            nst_ref[:, cols1] = u_c
        act = (_gelu_tanh(conv[0]) * conv[1]).astype(BF16)
        acc = acc + _dot(act, wdown_ref[c * FF_CHUNK:(c + 1) * FF_CHUNK, :])
    y_ref[...] = h + _rms(acc, gpost2_ref[...])


def _sample_tail(x, attn, state, w_in, b_in, ln_g, ln_b, ws0, bs0, g_att, g_gm, w_out, g_pre, g_post,
                 g_pre2, g_post2, w_up, conv_w, conv_b, w_down):
    S = x.shape[0]
    args = (x, attn, state, w_in, b_in, ln_g, ln_b, ws0, bs0, g_att, g_gm, w_out, g_pre, g_post,
            g_pre2, g_post2, w_up, conv_w, conv_b, w_down)
    return pl.pallas_call(
        _sample_tail_kernel,
        grid=(1,),
        in_specs=[_const_spec(a.shape) for a in args],
        out_specs=[
            pl.BlockSpec((S, D_MODEL), lambda i: (0, 0)),
            pl.BlockSpec((S, GMLP_W), lambda i: (0, 0)),
            pl.BlockSpec((S, 4 * D_FF), lambda i: (0, 0)),
        ],
        out_shape=[
            jax.ShapeDtypeStruct((S, D_MODEL), F32),
            jax.ShapeDtypeStruct((S, GMLP_W), F32),
            jax.ShapeDtypeStruct((S, 4 * D_FF), F32),
        ],
        compiler_params=pltpu.CompilerParams(
            dimension_semantics=("arbitrary",),
            vmem_limit_bytes=VMEM_LIMIT_BYTES),
        name="sample_tail",
    )(*args)


def kernel(x_prompt, x_sample, cache_win_k, cache_win_v, state_ffn_conv, rel_bias, w_in, b_in, attn_sinks, gmlp_ln_g, gmlp_ln_b, gmlp_w_s, gmlp_b_s, g_attn_out, g_gmlp_out, w_out, g_pre_mix, g_post_mix, g_pre_ffn, g_post_ffn, w_up, ffn_conv_w, ffn_conv_b, w_down):
    depth = w_in.shape[0]
    assert depth == 1, "single-layer step"
    B, L, _ = x_prompt.shape
    S = x_sample.shape[0]
    assert x_sample.shape[1] == 1 and L % PROMPT_TILE == 0 and S % SAMPLE_GROUP == 0
    assert cache_win_k.shape[2] == BLOCK

    row = lambda a: a.reshape(1, -1)
    w_in_b = w_in[0].astype(BF16)
    w_out_b = w_out[0].astype(BF16)
    b_in_r = row(b_in[0])
    w_up_b = w_up[0].astype(BF16)
    w_down_b = w_down[0].astype(BF16)
    conv_w, conv_b = ffn_conv_w[0], row(ffn_conv_b[0])
    bs_full = jnp.repeat(gmlp_b_s[0].T, HEAD_DIM, axis=1)
    ln_g, ln_b = row(gmlp_ln_g[0]), row(gmlp_ln_b[0])
    g_att, g_gm = row(g_attn_out[0]), row(g_gmlp_out[0])
    g_pre, g_post = row(g_pre_mix[0]), row(g_post_mix[0])
    g_pre2, g_post2 = row(g_pre_ffn[0]), row(g_post_ffn[0])

    y_p, wk_p, wv_p, cst_p = _prompt_layer(
        x_prompt, w_in_b, b_in_r, rel_bias, attn_sinks[0], ln_g, ln_b, gmlp_w_s[0], bs_full,
        g_att, g_gm, w_out_b, g_pre, g_post, w_up_b, conv_w, conv_b, w_down_b, g_pre2, g_post2)
    conv_p = cst_p[:, SUBLANES - 2:, :].reshape(1, B, 2, 2 * D_FF)

    xs = x_sample.reshape(S, D_MODEL)
    attn_s, wk_s, wv_s = _sample_attention(
        xs, w_in_b, b_in_r, g_pre,
        cache_win_k[0].reshape(S, BLOCK, KV_W), cache_win_v[0].reshape(S, BLOCK, KV_W),
        rel_bias.T, attn_sinks[0].reshape(N_HEADS, 1))
    ws0 = row(jnp.repeat(gmlp_w_s[0][:, 0, 0], HEAD_DIM))
    bs0 = row(jnp.repeat(gmlp_b_s[0][:, 0], HEAD_DIM))
    y_s, vn_s, nst_s = _sample_tail(
        xs, attn_s, state_ffn_conv[0].reshape(S, 4 * D_FF), w_in_b, b_in_r,
        ln_g, ln_b, ws0, bs0, g_att, g_gm, w_out_b, g_pre, g_post, g_pre2, g_post2,
        w_up_b, conv_w, conv_b, w_down_b)

    kv_shape_p = (1, B, BLOCK, N_KV, HEAD_DIM)
    kv_shape_s = (1, S, BLOCK, N_KV, HEAD_DIM)
    return (y_p,
            y_s.reshape(S, 1, D_MODEL),
            wk_p.reshape(kv_shape_p), wv_p.reshape(kv_shape_p),
            wk_s.reshape(kv_shape_s), wv_s.reshape(kv_shape_s),
            vn_s.reshape(1, S, 1, GMLP_W),
            conv_p,
            nst_s.reshape(1, S, 2, 2 * D_FF))
```

```python
import functools
import math

import numpy as np
import jax
import jax.numpy as jnp
from jax import lax
from jax.experimental import pallas as pl
from jax.experimental.pallas import tpu as pltpu

D_MODEL = 1024
HEAD_DIM = 64
N_HEADS = 8
N_KV = 2
N_PAIRS = N_HEADS // 2
BLOCK = 128
ATTN_W = N_HEADS * HEAD_DIM
KV_W = N_KV * HEAD_DIM
GMLP_W = 512
IN_W = ATTN_W + 2 * KV_W + 2 * GMLP_W
QKV_W = ATTN_W + 2 * KV_W
D_FF = 2816
FF_CHUNK = 256
N_FF_CHUNKS = D_FF // FF_CHUNK
NUM_BUCKETS = 32
MAX_DISTANCE = 128
EPS = 1e-6
NEG_INF = -1e30
SQRT_HALF = np.sqrt(0.5).astype(np.float32)
SQRT_2_OVER_PI = np.sqrt(2 / np.pi).astype(np.float32)

LANES = 128
SUBLANES = 8
VMEM_LIMIT_BYTES = 58 * 1024 * 1024

PROMPT_TILE = 512
SAMPLE_GROUP = 16

F32 = jnp.float32
BF16 = jnp.bfloat16


def _t5_bucket_np(dist):
    n = np.maximum(dist, 0)
    max_exact = NUM_BUCKETS // 2
    nf = np.maximum(n, 1).astype(np.float32)
    large = max_exact + (np.log(nf / max_exact) / math.log(MAX_DISTANCE / max_exact)
                         * (NUM_BUCKETS - max_exact)).astype(np.int32)
    large = np.minimum(large, NUM_BUCKETS - 1)
    return np.where(n < max_exact, n, large).astype(np.int32)


def _prompt_bucket_tile():
    r = np.arange(BLOCK)[:, None]
    c = np.arange(2 * BLOCK)[None, :]
    dist = r + BLOCK - c
    ok = (dist >= 0) & (dist <= BLOCK)
    return np.where(ok, _t5_bucket_np(dist), -1).astype(np.int32)


def _sample_bucket_row():
    c = np.arange(2 * BLOCK)[None, :]
    dist = BLOCK - c
    ok = dist >= 0
    return np.where(ok, _t5_bucket_np(dist), -1).astype(np.int32)


def _rms(x, g):
    ms = jnp.mean(x * x, axis=-1, keepdims=True)
    return x * lax.rsqrt(ms + EPS) * g


def _layer_norm(x, g, b):
    mu = jnp.mean(x, axis=-1, keepdims=True)
    xc = x - mu
    y = xc * lax.rsqrt(jnp.mean(xc * xc, axis=-1, keepdims=True) + EPS)
    return y * g + b


def _gelu_erf(x):
    return 0.5 * x * (1.0 + lax.erf(x * SQRT_HALF))


def _gelu_tanh(x):
    cdf = 0.5 * (1.0 + jnp.tanh(SQRT_2_OVER_PI * (x + 0.044715 * (x * x * x))))
    return x * cdf


def _softmax_with_sink(s, sink):
    m = jnp.maximum(jnp.max(s, axis=-1, keepdims=True), sink)
    p = jnp.exp(s - m)
    l = jnp.sum(p, axis=-1, keepdims=True) + jnp.exp(sink - m)
    return p, l


def _dot(a, b):
    return jnp.dot(a, b, preferred_element_type=F32)


def _dot_nt(a, b):
    return lax.dot_general(a, b, (((1,), (1,)), ((), ())), preferred_element_type=F32)


def _shift_rows(u, prev8, n):
    s = pltpu.roll(u, n, axis=0)
    top = jnp.where(lax.broadcasted_iota(jnp.int32, prev8.shape, 0) < n,
                    pltpu.roll(prev8, n, axis=0), s[0:SUBLANES])
    return jnp.concatenate([top, s[SUBLANES:]], axis=0)


def _ff_cols(gv, c):
    start = gv * D_FF + c * FF_CHUNK
    return slice(start, start + FF_CHUNK)


def _const_spec(shape):
    nd = len(shape)
    return pl.BlockSpec(shape, lambda *_: (0,) * nd, pipeline_mode=pl.Buffered(1))


def _smem_spec():
    return pl.BlockSpec(memory_space=pltpu.SMEM)


PROJ_PIECE = 256
N_PROJ_PIECES = IN_W // PROJ_PIECE
DOWN_PIECE = 256
N_DOWN_PIECES = D_MODEL // DOWN_PIECE
OUT_ROWS = 256


def _prompt_kernel(x_ref, win_ref, bin_ref, relb_ref, sink_ref, bucket_ref, lng_ref, lnb_ref,
                   ws_ref, bs_ref, gatt_ref, ggm_ref, wout_ref, gpre_ref, gpost_ref,
                   wupg_ref, wupv_ref, cw_ref, cb_ref, wdown_ref, gpre2_ref, gpost2_ref,
                   y_ref, wk_ref, wv_ref, cst_ref,
                   bias_s, xb_s, proj_s, q_s, kv_s, vnp_s, wsp_s, gmix_s, mix_s, h_s, h2_s, act_s, carry_s,
                   *, tile, n_tiles, n_total):
    step = pl.program_id(0)
    n_blk = tile // BLOCK
    t_mix = jnp.minimum(step, n_total - 1) % n_tiles
    t_ffn = jnp.maximum(step - 1, 0) % n_tiles
    slot = step % 2

    @pl.when(step == 0)
    def _init():
        bt = bucket_ref[...]
        for h in range(N_HEADS):
            acc = jnp.zeros((BLOCK, 2 * BLOCK), F32)
            for bk in range(NUM_BUCKETS):
                acc = jnp.where(bt == bk, relb_ref[bk, h], acc)
            bias_s[h] = acc
        row = lax.broadcasted_iota(jnp.int32, (BLOCK, BLOCK), 0)
        col = lax.broadcasted_iota(jnp.int32, (BLOCK, BLOCK), 1)
        tril = (row >= col).astype(F32)
        for p in range(N_PAIRS):
            wsp_s[p] = jnp.concatenate(
                [ws_ref[2 * p] * tril, ws_ref[2 * p + 1] * tril], axis=1).astype(BF16)
        kv_s[...] = jnp.zeros(kv_s.shape, BF16)
        carry_s[...] = jnp.zeros(carry_s.shape, F32)
        h_s[...] = jnp.zeros(h_s.shape, F32)

    held = {}

    def m_norm():
        prev = kv_s[:, tile:tile + BLOCK, :]
        kv_s[:, 0:BLOCK, :] = jnp.where(t_mix > 0, prev, jnp.zeros_like(prev))
        xb_s[...] = _rms(x_ref[0], gpre_ref[...]).astype(BF16)

    def m_proj(i):
        cols = slice(i * PROJ_PIECE, (i + 1) * PROJ_PIECE)
        proj_s[:, cols] = _dot(xb_s[...], win_ref[:, cols]) + bin_ref[:, cols]

    def m_q():
        q_s[...] = (proj_s[:, :ATTN_W] * (HEAD_DIM ** -0.5)).astype(BF16)

    def m_kv():
        k = proj_s[:, ATTN_W:ATTN_W + KV_W]
        v = proj_s[:, ATTN_W + KV_W:QKV_W]
        wk_ref[0] = k[tile - BLOCK:, :]
        wv_ref[0] = v[tile - BLOCK:, :]
        lo = lax.broadcasted_iota(jnp.int32, (tile, LANES), 1) < HEAD_DIM
        for base, val in ((0, k), (4, v)):
            rolled = pltpu.roll(val, HEAD_DIM, axis=1)
            kv_s[base + 0, BLOCK:, :] = jnp.where(lo, val, 0.0).astype(BF16)
            kv_s[base + 1, BLOCK:, :] = jnp.where(lo, 0.0, rolled).astype(BF16)
            kv_s[base + 2, BLOCK:, :] = jnp.where(lo, rolled, 0.0).astype(BF16)
            kv_s[base + 3, BLOCK:, :] = jnp.where(lo, 0.0, val).astype(BF16)

    def m_u():
        ucols = slice(QKV_W, QKV_W + GMLP_W)
        proj_s[:, ucols] = _gelu_erf(proj_s[:, ucols])

    def m_vg():
        vn = _layer_norm(_gelu_erf(proj_s[:, QKV_W + GMLP_W:]), lng_ref[...], lnb_ref[...])
        lo4 = (lax.broadcasted_iota(jnp.int32, (tile, GMLP_W), 1) & (LANES - 1)) < HEAD_DIM
        vnp_s[0] = jnp.where(lo4, vn, 0.0).astype(BF16)
        vnp_s[1] = jnp.where(lo4, 0.0, vn).astype(BF16)

    def m_gmlp():
        for p in range(N_PAIRS):
            cols = slice(p * LANES, (p + 1) * LANES)
            rhs = jnp.concatenate(
                [jnp.concatenate([vnp_s[0, j * BLOCK:(j + 1) * BLOCK, cols],
                                  vnp_s[1, j * BLOCK:(j + 1) * BLOCK, cols]], axis=0)
                 for j in range(n_blk)], axis=1)
            gmix_s[p] = _dot(wsp_s[p], rhs)

    def m_block(j):
        r = lax.broadcasted_iota(jnp.int32, (BLOCK, 2 * BLOCK), 0)
        c = lax.broadcasted_iota(jnp.int32, (BLOCK, 2 * BLOCK), 1)
        dist = r + BLOCK - c
        allowed = (dist >= 0) & (dist <= BLOCK)
        if j == 0:
            allowed = allowed & (c >= jnp.where(t_mix > 0, 0, BLOCK))
        lane_lo = lax.broadcasted_iota(jnp.int32, (BLOCK, LANES), 1) < HEAD_DIM
        rows = slice(j * BLOCK, (j + 1) * BLOCK)
        krows = slice(j * BLOCK, (j + 2) * BLOCK)

        attn_parts = []
        for g in range(N_KV):
            pairs = (2 * g, 2 * g + 1)
            qg = jnp.concatenate([q_s[rows, p * LANES:(p + 1) * LANES] for p in pairs], axis=0)
            kcat = jnp.concatenate([kv_s[2 * g, krows, :], kv_s[2 * g + 1, krows, :]], axis=0)
            s = _dot_nt(qg, kcat)
            probs, denoms = [], []
            for i, p in enumerate(pairs):
                pair_probs = []
                for par in range(2):
                    h = 2 * p + par
                    sh = s[i * BLOCK:(i + 1) * BLOCK, par * 2 * BLOCK:(par + 1) * 2 * BLOCK]
                    sh = jnp.where(allowed, sh + bias_s[h], NEG_INF)
                    ph, lh = _softmax_with_sink(sh, sink_ref[h])
                    pair_probs.append(ph.astype(BF16))
                    denoms.append(lh)
                probs.append(jnp.concatenate(pair_probs, axis=1))
            vcat = jnp.concatenate([kv_s[4 + 2 * g, krows, :], kv_s[4 + 2 * g + 1, krows, :]], axis=0)
            o = _dot(jnp.concatenate(probs, axis=0), vcat)
            for i in range(2):
                inv = jnp.where(lane_lo, 1.0 / denoms[2 * i], 1.0 / denoms[2 * i + 1])
                attn_parts.append(o[i * BLOCK:(i + 1) * BLOCK] * inv)

        gm_parts = []
        for p in range(N_PAIRS):
            cols = slice(p * LANES, (p + 1) * LANES)
            mixed = gmix_s[p, :, rows] + bs_ref[:, cols]
            ucols = slice(QKV_W + p * LANES, QKV_W + (p + 1) * LANES)
            gm_parts.append(proj_s[rows, ucols] * mixed)

        attn = jnp.concatenate(attn_parts, axis=1)
        gm = jnp.concatenate(gm_parts, axis=1)
        mix_s[rows, 0:ATTN_W] = _rms(attn, gatt_ref[...]).astype(BF16)
        mix_s[rows, ATTN_W:] = _rms(gm, ggm_ref[...]).astype(BF16)

    def m_out(i):
        rows = slice(i * OUT_ROWS, (i + 1) * OUT_ROWS)
        held["out", i] = _dot(mix_s[rows, :], wout_ref[...])

    def m_res(i):
        rows = slice(i * OUT_ROWS, (i + 1) * OUT_ROWS)
        h_s[slot, rows, :] = x_ref[0, rows, :] + _rms(held.pop(("out", i)), gpost_ref[...])

    def f_norm():
        h2_s[...] = _rms(h_s[1 - slot], gpre2_ref[...]).astype(BF16)

    def f_up(c):
        cols = slice(c * FF_CHUNK, (c + 1) * FF_CHUNK)
        w = jnp.concatenate([wupg_ref[:, cols], wupv_ref[:, cols]], axis=1)
        held["up", c] = _dot(h2_s[...], w)

    def f_gate(c):
        conv = []
        up = held.pop(("up", c))
        for gv in range(2):
            u = up[:, gv * FF_CHUNK:(gv + 1) * FF_CHUNK]
            cols = _ff_cols(gv, c)
            prev8 = jnp.where(t_ffn > 0, carry_s[:, cols], 0.0)
            conv.append(cb_ref[:, cols] + cw_ref[0:1, cols] * _shift_rows(u, prev8, 2)
                        + cw_ref[1:2, cols] * _shift_rows(u, prev8, 1) + cw_ref[2:3, cols] * u)
            carry_s[:, cols] = u[tile - SUBLANES:, :]
        act_s[:, c * FF_CHUNK:(c + 1) * FF_CHUNK] = (_gelu_tanh(conv[0]) * conv[1]).astype(BF16)

    def f_down(n):
        held["down", n] = _dot(act_s[...], wdown_ref[:, n * DOWN_PIECE:(n + 1) * DOWN_PIECE])

    def f_res():
        f = jnp.concatenate([held.pop(("down", n)) for n in range(N_DOWN_PIECES)], axis=1)
        cst_ref[0] = carry_s[...]
        y_ref[0] = h_s[1 - slot] + _rms(f, gpost2_ref[...])

    assert n_blk == 4 and N_DOWN_PIECES == 4 and tile == 2 * OUT_ROWS and N_PROJ_PIECES <= N_FF_CHUNKS
    q0, q1, kv, u0, u1, vg0, vg1 = range(N_PROJ_PIECES)
    proj_order = [vg0, vg1, u0, u1, kv, q0, q1]
    slot_vector = {2: [m_vg], 3: [m_gmlp], 4: [m_u], 5: [m_kv], 7: [m_q]}

    def emit(mixer_on, ffn_on):
        def m(item, *a):
            if mixer_on:
                item(*a)

        def f(item, *a):
            if ffn_on:
                item(*a)

        f(f_norm)
        m(m_norm)
        f(f_up, 0)
        for c in range(N_FF_CHUNKS):
            if c + 1 < N_FF_CHUNKS:
                f(f_up, c + 1)
            if c < len(proj_order):
                m(m_proj, proj_order[c])
            f(f_gate, c)
            for item in slot_vector.get(c, []):
                m(item)
        f(f_down, 0)
        f(f_down, 1)
        m(m_block, 0)
        f(f_down, 2)
        m(m_block, 1)
        m(m_out, 0)
        f(f_down, 3)
        m(m_block, 2)
        m(m_block, 3)
        m(m_res, 0)
        m(m_out, 1)
        f(f_res)
        m(m_res, 1)

    emit(True, True)


def _prompt_layer(x, w_in, b_in, rel_bias, sinks, ln_g, ln_b, w_s, bs_full, g_att, g_gm, w_out,
                  g_pre, g_post, w_up_gate, w_up_val, conv_w, conv_b, w_down, g_pre2, g_post2):
    B, L, _ = x.shape
    tile = PROMPT_TILE
    n_tiles = L // tile
    n_total = B * n_tiles
    bucket = jnp.asarray(_prompt_bucket_tile())

    def mix_tile(s):
        return jnp.minimum(s, n_total - 1)

    def ffn_tile(s):
        return jnp.maximum(s - 1, 0)

    kern = functools.partial(_prompt_kernel, tile=tile, n_tiles=n_tiles, n_total=n_total)
    return pl.pallas_call(
        kern,
        grid=(n_total + 1,),
        in_specs=[
            pl.BlockSpec((1, tile, D_MODEL), lambda s: (mix_tile(s) // n_tiles, mix_tile(s) % n_tiles, 0)),
            _const_spec((D_MODEL, IN_W)),
            _const_spec((1, IN_W)),
            _smem_spec(),
            _smem_spec(),
            _const_spec((BLOCK, 2 * BLOCK)),
            _const_spec((1, GMLP_W)),
            _const_spec((1, GMLP_W)),
            _const_spec((N_HEADS, BLOCK, BLOCK)),
            _const_spec((BLOCK, GMLP_W)),
            _const_spec((1, ATTN_W)),
            _const_spec((1, GMLP_W)),
            _const_spec((D_MODEL, D_MODEL)),
            _const_spec((1, D_MODEL)),
            _const_spec((1, D_MODEL)),
            _const_spec((D_MODEL, D_FF)),
            _const_spec((D_MODEL, D_FF)),
            _const_spec((3, 2 * D_FF)),
            _const_spec((1, 2 * D_FF)),
            _const_spec((D_FF, D_MODEL)),
            _const_spec((1, D_MODEL)),
            _const_spec((1, D_MODEL)),
        ],
        out_specs=[
            pl.BlockSpec((1, tile, D_MODEL), lambda s: (ffn_tile(s) // n_tiles, ffn_tile(s) % n_tiles, 0)),
            pl.BlockSpec((1, BLOCK, KV_W), lambda s: (mix_tile(s) // n_tiles, 0, 0)),
            pl.BlockSpec((1, BLOCK, KV_W), lambda s: (mix_tile(s) // n_tiles, 0, 0)),
            pl.BlockSpec((1, SUBLANES, 2 * D_FF), lambda s: (ffn_tile(s) // n_tiles, 0, 0)),
        ],
        out_shape=[
            jax.ShapeDtypeStruct((B, L, D_MODEL), F32),
            jax.ShapeDtypeStruct((B, BLOCK, KV_W), F32),
            jax.ShapeDtypeStruct((B, BLOCK, KV_W), F32),
            jax.ShapeDtypeStruct((B, SUBLANES, 2 * D_FF), F32),
        ],
        scratch_shapes=[
            pltpu.VMEM((N_HEADS, BLOCK, 2 * BLOCK), F32),
            pltpu.VMEM((tile, D_MODEL), BF16),
            pltpu.VMEM((tile, IN_W), F32),
            pltpu.VMEM((tile, ATTN_W), BF16),
            pltpu.VMEM((8, tile + BLOCK, LANES), BF16),
            pltpu.VMEM((2, tile, GMLP_W), BF16),
            pltpu.VMEM((N_PAIRS, BLOCK, 2 * BLOCK), BF16),
            pltpu.VMEM((N_PAIRS, BLOCK, tile), F32),
            pltpu.VMEM((tile, D_MODEL), BF16),
            pltpu.VMEM((2, tile, D_MODEL), F32),
            pltpu.VMEM((tile, D_MODEL), BF16),
            pltpu.VMEM((tile, D_FF), BF16),
            pltpu.VMEM((SUBLANES, 2 * D_FF), F32),
        ],
        compiler_params=pltpu.CompilerParams(
            dimension_semantics=("arbitrary",),
            vmem_limit_bytes=VMEM_LIMIT_BYTES),
        name="prompt_layer",
    )(x, w_in, b_in, rel_bias, sinks, bucket, ln_g, ln_b, w_s, bs_full, g_att, g_gm, w_out,
      g_pre, g_post, w_up_gate, w_up_val, conv_w, conv_b, w_down, g_pre2, g_post2)


def _sample_attn_kernel(x_ref, wqkv_ref, bqkv_ref, gpre_ref, ck_ref, cv_ref, relbt_ref, sink_ref,
                        bucket_ref, attn_ref, wk_ref, wv_ref,
                        bias_s, kext_s, vext_s, rep_s, wqkvb_s, *, group):
    i = pl.program_id(0)
    rows = group * SUBLANES

    @pl.when(i == 0)
    def _init():
        wqkvb_s[...] = wqkv_ref[...].astype(BF16)
        bt = jnp.broadcast_to(bucket_ref[...], (N_HEADS, 2 * BLOCK))
        acc = jnp.zeros((N_HEADS, 2 * BLOCK), F32)
        for bk in range(NUM_BUCKETS):
            acc = jnp.where(bt == bk, relbt_ref[:, bk:bk + 1], acc)
        bias_s[...] = acc
        kext_s[...] = jnp.zeros(kext_s.shape, F32)
        vext_s[...] = jnp.zeros(vext_s.shape, F32)

    xn = _rms(x_ref[...], gpre_ref[...])
    xr = jnp.concatenate(
        [jnp.broadcast_to(xn[s:s + 1, :], (SUBLANES, D_MODEL)) for s in range(group)], axis=0)
    proj = _dot(xr.astype(BF16), wqkvb_s[...]) + bqkv_ref[...]
    q = proj[:, :ATTN_W] * (HEAD_DIM ** -0.5)
    k3 = proj[:, ATTN_W:ATTN_W + KV_W].reshape(group, SUBLANES, LANES)
    v3 = proj[:, ATTN_W + KV_W:].reshape(group, SUBLANES, LANES)

    head = lax.broadcasted_iota(jnp.int32, (rows, LANES), 0) & (SUBLANES - 1)
    lane_half = lax.broadcasted_iota(jnp.int32, (rows, LANES), 1) // HEAD_DIM
    lane_lo = lane_half == 0
    qpad = jnp.zeros((rows, LANES), F32)
    for p in range(N_PAIRS):
        g = p // 2
        qp = q[:, p * LANES:(p + 1) * LANES]
        qr = pltpu.roll(qp, HEAD_DIM, axis=1)
        src_even = qp if g == 0 else qr
        src_odd = qr if g == 0 else qp
        qpad = jnp.where((head == 2 * p) & (lane_half == g), src_even, qpad)
        qpad = jnp.where((head == 2 * p + 1) & (lane_half == g), src_odd, qpad)
    qpad = qpad.reshape(group, SUBLANES, LANES).astype(BF16)

    ck = ck_ref[...]
    cv = cv_ref[...]
    kext_s[:, 0:BLOCK, :] = ck
    vext_s[:, 0:BLOCK, :] = cv
    kext_s[:, BLOCK:BLOCK + SUBLANES, :] = k3
    vext_s[:, BLOCK:BLOCK + SUBLANES, :] = v3

    s = jnp.einsum('shc,sjc->shj', qpad, kext_s[...].astype(BF16), preferred_element_type=F32)
    allowed = jnp.broadcast_to(bucket_ref[...], (N_HEADS, 2 * BLOCK)) >= 0
    s = jnp.where(allowed[None], s + bias_s[...][None], NEG_INF)
    pr, l = _softmax_with_sink(s, sink_ref[...][None])
    o = jnp.einsum('shj,sjc->shc', pr.astype(BF16), vext_s[...].astype(BF16),
                   preferred_element_type=F32)
    o = (o * (1.0 / l)).reshape(rows, LANES)

    om = jnp.where(lane_half == head // (N_HEADS // N_KV), o, 0.0)
    both = (om + pltpu.roll(om, HEAD_DIM, axis=1)).reshape(group, SUBLANES, LANES)
    lane_lo3 = lane_lo.reshape(group, SUBLANES, LANES)
    for p in range(N_PAIRS):
        even = jnp.broadcast_to(both[:, 2 * p:2 * p + 1, :], (group, SUBLANES, LANES))
        odd = jnp.broadcast_to(both[:, 2 * p + 1:2 * p + 2, :], (group, SUBLANES, LANES))
        rep_s[p] = jnp.where(lane_lo3, even, odd).reshape(rows, LANES)
    for p in range(N_PAIRS):
        attn_ref[:, p * LANES:(p + 1) * LANES] = rep_s[p, pl.ds(0, group, stride=SUBLANES), :]

    last = lax.broadcasted_iota(jnp.int32, (group, SUBLANES, LANES), 1) == SUBLANES - 1
    for cache, new3, out_ref in ((ck, k3, wk_ref), (cv, v3, wv_ref)):
        shifted = pltpu.roll(cache, BLOCK - 1, axis=1)
        out_ref[:, 0:BLOCK - SUBLANES, :] = shifted[:, 0:BLOCK - SUBLANES, :]
        out_ref[:, BLOCK - SUBLANES:, :] = jnp.where(last, new3, shifted[:, BLOCK - SUBLANES:, :])


def _sample_attention(x, wqkv, bqkv, g_pre, cache_k, cache_v, relb_t, sinks_col):
    S = x.shape[0]
    group = SAMPLE_GROUP
    bucket = jnp.asarray(_sample_bucket_row())
    kern = functools.partial(_sample_attn_kernel, group=group)
    return pl.pallas_call(
        kern,
        grid=(S // group,),
        in_specs=[
            pl.BlockSpec((group, D_MODEL), lambda i: (i, 0)),
            _const_spec((D_MODEL, QKV_W)),
            _const_spec((1, QKV_W)),
            _const_spec((1, D_MODEL)),
            pl.BlockSpec((group, BLOCK, KV_W), lambda i: (i, 0, 0)),
            pl.BlockSpec((group, BLOCK, KV_W), lambda i: (i, 0, 0)),
            _const_spec((N_HEADS, NUM_BUCKETS)),
            _const_spec((N_HEADS, 1)),
            _const_spec((1, 2 * BLOCK)),
        ],
        out_specs=[
            pl.BlockSpec((group, ATTN_W), lambda i: (i, 0)),
            pl.BlockSpec((group, BLOCK, KV_W), lambda i: (i, 0, 0)),
            pl.BlockSpec((group, BLOCK, KV_W), lambda i: (i, 0, 0)),
        ],
        out_shape=[
            jax.ShapeDtypeStruct((S, ATTN_W), F32),
            jax.ShapeDtypeStruct((S, BLOCK, KV_W), F32),
            jax.ShapeDtypeStruct((S, BLOCK, KV_W), F32),
        ],
        scratch_shapes=[
            pltpu.VMEM((N_HEADS, 2 * BLOCK), F32),
            pltpu.VMEM((group, 2 * BLOCK, LANES), F32),
            pltpu.VMEM((group, 2 * BLOCK, LANES), F32),
            pltpu.VMEM((N_PAIRS, group * SUBLANES, LANES), F32),
            pltpu.VMEM((D_MODEL, QKV_W), BF16),
        ],
        compiler_params=pltpu.CompilerParams(
            dimension_semantics=("arbitrary",),
            vmem_limit_bytes=VMEM_LIMIT_BYTES),
        name="sample_attention",
    )(x, wqkv, bqkv, g_pre, cache_k, cache_v, relb_t, sinks_col, bucket)


def _sample_tail_kernel(x_ref, attn_ref, st_ref, win_ref, bin_ref, lng_ref, lnb_ref, ws0_ref, bs0_ref,
                        gatt_ref, ggm_ref, wout_ref, gpre_ref, gpost_ref, gpre2_ref, gpost2_ref,
                        wupg_ref, wupv_ref, cw_ref, cb_ref, wdown_ref,
                        y_ref, vn_ref, nst_ref, winb_ref, woutb_ref, wupgb_ref, wupvb_ref, wdownb_ref,
                        h_s, h2_s, acc_s):
    c = pl.program_id(0)
    two_ff = 2 * D_FF

    @pl.when(c == 0)
    def _mixer_tail():
        win_b = win_ref[...].astype(BF16)
        winb_ref[...] = win_b
        wout_b = wout_ref[...].astype(BF16)
        woutb_ref[...] = wout_b
        x = x_ref[...]
        xn = _rms(x, gpre_ref[...])
        puv = _dot(xn.astype(BF16), win_b[:, QKV_W:]) + bin_ref[:, QKV_W:]
        u = _gelu_erf(puv[:, :GMLP_W])
        vn = _layer_norm(_gelu_erf(puv[:, GMLP_W:]), lng_ref[...], lnb_ref[...])
        vn_ref[...] = vn
        gm = u * (ws0_ref[...] * vn + bs0_ref[...])
        mix = jnp.concatenate([_rms(attn_ref[...], gatt_ref[...]), _rms(gm, ggm_ref[...])], axis=1)
        h = x + _rms(_dot(mix.astype(BF16), wout_b), gpost_ref[...])
        h_s[...] = h
        h2_s[...] = _rms(h, gpre2_ref[...]).astype(BF16)
        acc_s[...] = jnp.zeros(acc_s.shape, F32)
        nst_ref[:, 0:two_ff] = st_ref[:, two_ff:]

    wd_b = wdown_ref[...].astype(BF16)
    wdownb_ref[...] = wd_b
    conv = []
    for gv, (w_ref, wb_ref) in enumerate(((wupg_ref, wupgb_ref), (wupv_ref, wupvb_ref))):
        w_b = w_ref[...].astype(BF16)
        wb_ref[...] = w_b
        u_c = _dot(h2_s[...], w_b)
        col0 = pl.ds(pl.multiple_of(gv * D_FF + c * FF_CHUNK, LANES), FF_CHUNK)
        col1 = pl.ds(pl.multiple_of(two_ff + gv * D_FF + c * FF_CHUNK, LANES), FF_CHUNK)
        conv.append(cb_ref[:, col0] + cw_ref[0:1, col0] * st_ref[:, col0]
                    + cw_ref[1:2, col0] * st_ref[:, col1] + cw_ref[2:3, col0] * u_c)
        nst_ref[:, col1] = u_c
    act = (_gelu_tanh(conv[0]) * conv[1]).astype(BF16)
    acc_s[...] += _dot(act, wd_b)

    @pl.when(c == N_FF_CHUNKS - 1)
    def _finish():
        y_ref[...] = h_s[...] + _rms(acc_s[...], gpost2_ref[...])


def _sample_tail(x, attn, state, w_in, b_in, ln_g, ln_b, ws0, bs0, g_att, g_gm, w_out, g_pre, g_post,
                 g_pre2, g_post2, w_up, conv_w, conv_b, w_down):
    S = x.shape[0]
    consts_a = (x, attn, state, w_in, b_in, ln_g, ln_b, ws0, bs0, g_att, g_gm, w_out, g_pre, g_post,
                g_pre2, g_post2)
    n_val = D_FF // FF_CHUNK
    return pl.pallas_call(
        _sample_tail_kernel,
        grid=(N_FF_CHUNKS,),
        in_specs=[_const_spec(a.shape) for a in consts_a] + [
            pl.BlockSpec((D_MODEL, FF_CHUNK), lambda c: (0, c)),
            pl.BlockSpec((D_MODEL, FF_CHUNK), lambda c: (0, n_val + c)),
            _const_spec(conv_w.shape),
            _const_spec(conv_b.shape),
            pl.BlockSpec((FF_CHUNK, D_MODEL), lambda c: (c, 0)),
        ],
        out_specs=[
            pl.BlockSpec((S, D_MODEL), lambda c: (0, 0)),
            pl.BlockSpec((S, GMLP_W), lambda c: (0, 0)),
            pl.BlockSpec((S, 4 * D_FF), lambda c: (0, 0)),
            pl.BlockSpec((D_MODEL, IN_W), lambda c: (0, 0)),
            pl.BlockSpec((D_MODEL, D_MODEL), lambda c: (0, 0)),
            pl.BlockSpec((D_MODEL, FF_CHUNK), lambda c: (0, c)),
            pl.BlockSpec((D_MODEL, FF_CHUNK), lambda c: (0, c)),
            pl.BlockSpec((FF_CHUNK, D_MODEL), lambda c: (c, 0)),
        ],
        out_shape=[
            jax.ShapeDtypeStruct((S, D_MODEL), F32),
            jax.ShapeDtypeStruct((S, GMLP_W), F32),
            jax.ShapeDtypeStruct((S, 4 * D_FF), F32),
            jax.ShapeDtypeStruct((D_MODEL, IN_W), BF16),
            jax.ShapeDtypeStruct((D_MODEL, D_MODEL), BF16),
            jax.ShapeDtypeStruct((D_MODEL, D_FF), BF16),
            jax.ShapeDtypeStruct((D_MODEL, D_FF), BF16),
            jax.ShapeDtypeStruct((D_FF, D_MODEL), BF16),
        ],
        scratch_shapes=[
            pltpu.VMEM((S, D_MODEL), F32),
            pltpu.VMEM((S, D_MODEL), BF16),
            pltpu.VMEM((S, D_MODEL), F32),
        ],
        compiler_params=pltpu.CompilerParams(
            dimension_semantics=("arbitrary",),
            vmem_limit_bytes=VMEM_LIMIT_BYTES),
        name="sample_tail",
    )(*consts_a, w_up, w_up, conv_w, conv_b, w_down)


def kernel(x_prompt, x_sample, cache_win_k, cache_win_v, state_ffn_conv, rel_bias, w_in, b_in, attn_sinks, gmlp_ln_g, gmlp_ln_b, gmlp_w_s, gmlp_b_s, g_attn_out, g_gmlp_out, w_out, g_pre_mix, g_post_mix, g_pre_ffn, g_post_ffn, w_up, ffn_conv_w, ffn_conv_b, w_down):
    depth = w_in.shape[0]
    assert depth == 1, "single-layer step"
    B, L, _ = x_prompt.shape
    S = x_sample.shape[0]
    assert x_sample.shape[1] == 1 and L % PROMPT_TILE == 0 and S % SAMPLE_GROUP == 0
    assert cache_win_k.shape[2] == BLOCK

    row = lambda a: a.reshape(1, -1)
    b_in_r = row(b_in[0])
    conv_w, conv_b = ffn_conv_w[0], row(ffn_conv_b[0])
    bs_full = jnp.repeat(gmlp_b_s[0].T, HEAD_DIM, axis=1)
    ln_g, ln_b = row(gmlp_ln_g[0]), row(gmlp_ln_b[0])
    g_att, g_gm = row(g_attn_out[0]), row(g_gmlp_out[0])
    g_pre, g_post = row(g_pre_mix[0]), row(g_post_mix[0])
    g_pre2, g_post2 = row(g_pre_ffn[0]), row(g_post_ffn[0])

    xs = x_sample.reshape(S, D_MODEL)
    attn_s, wk_s, wv_s = _sample_attention(
        xs, w_in[0], b_in_r, g_pre,
        cache_win_k[0].reshape(S, BLOCK, KV_W), cache_win_v[0].reshape(S, BLOCK, KV_W),
        rel_bias.T, attn_sinks[0].reshape(N_HEADS, 1))
    ws0 = row(jnp.repeat(gmlp_w_s[0][:, 0, 0], HEAD_DIM))
    bs0 = row(jnp.repeat(gmlp_b_s[0][:, 0], HEAD_DIM))
    y_s, vn_s, nst_s, w_in_b, w_out_b, w_up_gate_b, w_up_val_b, w_down_b = _sample_tail(
        xs, attn_s, state_ffn_conv[0].reshape(S, 4 * D_FF), w_in[0], b_in_r,
        ln_g, ln_b, ws0, bs0, g_att, g_gm, w_out[0], g_pre, g_post, g_pre2, g_post2,
        w_up[0], conv_w, conv_b, w_down[0])

    y_p, wk_p, wv_p, cst_p = _prompt_layer(
        x_prompt, w_in_b, b_in_r, rel_bias, attn_sinks[0], ln_g, ln_b, gmlp_w_s[0], bs_full,
        g_att, g_gm, w_out_b, g_pre, g_post, w_up_gate_b, w_up_val_b, conv_w, conv_b, w_down_b,
        g_pre2, g_post2)
    conv_p = cst_p[:, SUBLANES - 2:, :].reshape(1, B, 2, 2 * D_FF)

    kv_shape_p = (1, B, BLOCK, N_KV, HEAD_DIM)
    kv_shape_s = (1, S, BLOCK, N_KV, HEAD_DIM)
    return (y_p,
            y_s.reshape(S, 1, D_MODEL),
            wk_p.reshape(kv_shape_p), wv_p.reshape(kv_shape_p),
            wk_s.reshape(kv_shape_s), wv_s.reshape(kv_shape_s),
            vn_s.reshape(1, S, 1, GMLP_W),
            conv_p,
            nst_s.reshape(1, S, 2, 2 * D_FF))
```
